```python
import math
import jax, jax.numpy as jnp
from jax import lax
import numpy as np

D_MODEL = 1024
BATCH = 8
SEQ = 2048
DEPTH = 4
DEC_BATCH = 128
DEC_SEQ = 1
PAST_LEN = 16384
PAGE_SIZE = 128

N_EVEN = (DEPTH + 1) // 2
N_ODD = DEPTH // 2
POOL_WINDOWS = (2, 4, 8, 16)
N_POOL_GROUPS = len(POOL_WINDOWS)
D_POOL = D_MODEL
POOL_GROUP = D_POOL // N_POOL_GROUPS
POOL_BUF = max(POOL_WINDOWS) - 1
D_CONV = D_MODEL
CONV_K = 3
EVEN_IN = 2 * D_POOL + 4 * D_CONV
EVEN_MIX = D_POOL + D_CONV
SSM_EXPAND = 2
D_INNER = SSM_EXPAND * D_MODEL
SSM_HEAD_DIM = 64
SSM_HEADS = D_INNER // SSM_HEAD_DIM
SSM_GROUPS = 4
SSM_HEADS_PER_GROUP = SSM_HEADS // SSM_GROUPS
SSM_STATE = 128
SSM_CONV_K = 4
SSM_CONV_DIM = D_INNER + 2 * SSM_GROUPS * SSM_STATE
ODD_IN = D_INNER + SSM_CONV_DIM + SSM_HEADS
SSM_CHUNK = 128
N_MEM = 256
XA_HEADS = 4
XA_HEAD_DIM = D_MODEL // XA_HEADS
EPS = 1e-6

kernel_name = 'pool_conv_ssd_memxattn_decoder_step'


def rmsnorm(x, g):
    xf = x.astype(jnp.float32)
    y = xf * lax.rsqrt(jnp.mean(xf * xf, axis=-1, keepdims=True) + EPS)
    return (y * g.astype(jnp.float32)).astype(x.dtype)


def causal_dwconv(v, prefix, w):
    K = w.shape[0]
    L = v.shape[1]
    ext = jnp.concatenate([prefix.astype(v.dtype), v], axis=1)
    out = ext[:, 0:L] * w[0]
    for k in range(1, K):
        out = out + ext[:, k:k + L] * w[k]
    return out, ext[:, ext.shape[1] - (K - 1):]


def pool_mixer(u, prefix, pos0, w_grp, scale):
    b, L, _ = u.shape
    f32 = jnp.float32
    ext = jnp.concatenate([prefix.astype(u.dtype), u], axis=1).astype(f32)
    cs = jnp.concatenate([jnp.zeros((b, 1, D_POOL), f32), jnp.cumsum(ext, axis=1)], axis=1)
    pos = pos0 + jnp.arange(L)
    end = cs[:, POOL_BUF + 1:]
    uf = u.astype(f32)
    outs = []
    for g, w in enumerate(POOL_WINDOWS):
        sl = slice(g * POOL_GROUP, (g + 1) * POOL_GROUP)
        start = cs[:, POOL_BUF + 1 - w:POOL_BUF + 1 - w + L, sl]
        cnt = jnp.minimum(pos + 1, w).astype(f32)[None, :, None]
        diff = ((end[..., sl] - start) / cnt - uf[..., sl]).astype(u.dtype)
        outs.append(diff @ w_grp[g])
    y = jnp.concatenate(outs, axis=-1) * scale
    return y, ext[:, ext.shape[1] - POOL_BUF:].astype(u.dtype)


def even_mixer(h, pool_prefix, conv_prefix, pos0, w_in, pool_w, pool_scale, conv_w, w_out):
    proj = h @ w_in
    c0 = D_POOL
    c1 = c0 + D_POOL
    c2 = c1 + D_CONV
    c3 = c2 + D_CONV
    c4 = c3 + D_CONV
    u, g_pool = proj[..., :c0], proj[..., c0:c1]
    b_gate, c_gate, v, g_conv = proj[..., c1:c2], proj[..., c2:c3], proj[..., c3:c4], proj[..., c4:]
    y_pool, pool_buf = pool_mixer(u, pool_prefix, pos0, pool_w, pool_scale)
    y_pool = y_pool * jax.nn.silu(g_pool)
    conv_out, conv_buf = causal_dwconv(c_gate * v, conv_prefix, conv_w)
    y_conv = b_gate * conv_out * jax.nn.silu(g_conv)
    return jnp.concatenate([y_pool, y_conv], axis=-1) @ w_out, pool_buf, conv_buf


def ssd_scan(x, dt, A, Bm, Cm, s0):
    b, L = x.shape[:2]
    Q = SSM_CHUNK if L % SSM_CHUNK == 0 else L
    nc = L // Q
    x = x.reshape((b, nc, Q) + x.shape[2:])
    dt = dt.reshape((b, nc, Q) + dt.shape[2:])
    Bm = Bm.reshape((b, nc, Q) + Bm.shape[2:])
    Cm = Cm.reshape((b, nc, Q) + Cm.shape[2:])
    a_cs = jnp.cumsum(dt * A, axis=2)
    xdt = x * dt[..., None]
    causal = jnp.tril(jnp.ones((Q, Q), bool))[:, :, None, None]
    seg = a_cs[:, :, :, None] - a_cs[:, :, None, :]
    decay = jnp.exp(jnp.where(causal, seg, -jnp.inf))
    cb = jnp.einsum('bclgn,bcsgn->bclsg', Cm, Bm)
    y_diag = jnp.einsum('bclsge,bcsgep->bclgep', cb[..., None] * decay, xdt)
    to_end = jnp.exp(a_cs[:, :, -1:] - a_cs)
    chunk_states = jnp.einsum('bcsgn,bcsgep->bcgepn', Bm, xdt * to_end[..., None])
    chunk_decay = jnp.exp(a_cs[:, :, -1])

    def step(s, inp):
        dec, st = inp
        return dec[..., None, None] * s + st, s

    s_fin, s_in = lax.scan(step, s0, (jnp.moveaxis(chunk_decay, 1, 0), jnp.moveaxis(chunk_states, 1, 0)))
    s_in = jnp.moveaxis(s_in, 0, 1)
    y_off = jnp.einsum('bclgn,bcgepn->bclgep', Cm, s_in) * jnp.exp(a_cs)[..., None]
    y = (y_diag + y_off).reshape((b, L) + x.shape[3:])
    return y, s_fin


def mamba2_mixer(h, conv_prefix, s0, w_in, conv_w, conv_b, dt_bias, a_log, d_skip, norm_w, w_out):
    b, L, _ = h.shape
    f32 = jnp.float32
    G, E, P, N = SSM_GROUPS, SSM_HEADS_PER_GROUP, SSM_HEAD_DIM, SSM_STATE
    proj = h @ w_in
    z = proj[..., :D_INNER]
    xbc = proj[..., D_INNER:D_INNER + SSM_CONV_DIM]
    dt_raw = proj[..., D_INNER + SSM_CONV_DIM:]
    xbc_c, conv_buf = causal_dwconv(xbc, conv_prefix, conv_w)
    xbc_c = jax.nn.silu(xbc_c + conv_b).astype(f32)
    gn = G * N
    xs = xbc_c[..., :D_INNER].reshape(b, L, G, E, P)
    Bm = xbc_c[..., D_INNER:D_INNER + gn].reshape(b, L, G, N)
    Cm = xbc_c[..., D_INNER + gn:].reshape(b, L, G, N)
    dt = jax.nn.softplus(dt_raw.astype(f32) + dt_bias.astype(f32)).reshape(b, L, G, E)
    A = -jnp.exp(a_log.astype(f32)).reshape(G, E)
    y, s_fin = ssd_scan(xs, dt, A, Bm, Cm, s0.astype(f32).reshape(b, G, E, P, N))
    y = y + d_skip.astype(f32).reshape(G, E)[..., None] * xs
    yg = y.reshape(b, L, G, E * P) * jax.nn.silu(z.astype(f32)).reshape(b, L, G, E * P)
    yg = yg * lax.rsqrt(jnp.mean(yg * yg, axis=-1, keepdims=True) + EPS)
    yg = (yg.reshape(b, L, D_INNER) * norm_w.astype(f32)).astype(h.dtype)
    return yg @ w_out, conv_buf, s_fin.reshape(b, SSM_HEADS, P, N).astype(h.dtype)


def mem_xattn(h, mk, mv, wq, wo):
    b, L, _ = h.shape
    q = (h @ wq).reshape(b, L, XA_HEADS, XA_HEAD_DIM).astype(jnp.float32)
    s = jnp.einsum('blhd,bmhd->bhlm', q, mk.astype(jnp.float32)) * (XA_HEAD_DIM ** -0.5)
    p = jax.nn.softmax(s, axis=-1)
    o = jnp.einsum('bhlm,bmhd->blhd', p, mv.astype(jnp.float32)).astype(h.dtype).reshape(b, L, D_MODEL)
    return o @ wo


def run_trunk(x, mem_k, mem_v, pool_st, conv_st, sconv_st, ssm_st, pos0, weights):
    (norm_mix, norm_xa, norm_final, w_in_even, pool_w, pool_scale, conv_w, w_out_even,
     w_in_odd, ssm_conv_w, ssm_conv_b, dt_bias, a_log, d_skip, ssm_norm, w_out_odd,
     w_xa_q, w_xa_o) = weights
    new_pool, new_conv, new_sconv, new_ssm = [], [], [], []
    for l in range(DEPTH):
        i = l // 2
        h = rmsnorm(x, norm_mix[l])
        if l % 2 == 0:
            out, pb, cbuf = even_mixer(h, pool_st[i], conv_st[i], pos0, w_in_even[i], pool_w[i],
                                       pool_scale[i], conv_w[i], w_out_even[i])
            new_pool.append(pb)
            new_conv.append(cbuf)
        else:
            out, sb, ss = mamba2_mixer(h, sconv_st[i], ssm_st[i], w_in_odd[i], ssm_conv_w[i], ssm_conv_b[i],
                                       dt_bias[i], a_log[i], d_skip[i], ssm_norm[i], w_out_odd[i])
            new_sconv.append(sb)
            new_ssm.append(ss)
        x = x + out
        x = x + mem_xattn(rmsnorm(x, norm_xa[l]), mem_k[l], mem_v[l], w_xa_q[l], w_xa_o[l])
    y = rmsnorm(x, norm_final)
    return y, jnp.stack(new_pool), jnp.stack(new_conv), jnp.stack(new_sconv), jnp.stack(new_ssm)


def setup_inputs(seed: int = 0) -> dict:
    key = jax.random.key(seed)
    ks = iter(jax.random.split(key, 32))
    f32 = jnp.float32

    def nrm(shape, scale):
        return jax.random.normal(next(ks), shape, f32) * scale

    x_prompt = nrm((BATCH, SEQ, D_MODEL), 1.0)
    x_sample = nrm((DEC_BATCH, DEC_SEQ, D_MODEL), 1.0)
    mem_prompt = nrm((BATCH, N_MEM, D_MODEL), 1.0)
    cache_mem_k = nrm((DEPTH, DEC_BATCH, N_MEM, XA_HEADS, XA_HEAD_DIM), 1.0)
    cache_mem_v = nrm((DEPTH, DEC_BATCH, N_MEM, XA_HEADS, XA_HEAD_DIM), 1.0)
    state_pool = nrm((N_EVEN, DEC_BATCH, POOL_BUF, D_POOL), 1.0)
    state_conv = nrm((N_EVEN, DEC_BATCH, CONV_K - 1, D_CONV), 1.0)
    state_ssm_conv = nrm((N_ODD, DEC_BATCH, SSM_CONV_K - 1, SSM_CONV_DIM), 1.0)
    state_ssm = nrm((N_ODD, DEC_BATCH, SSM_HEADS, SSM_HEAD_DIM, SSM_STATE), 0.1)

    norm_mix = 1.0 + nrm((DEPTH, D_MODEL), 0.02)
    norm_xa = 1.0 + nrm((DEPTH, D_MODEL), 0.02)
    norm_final = 1.0 + nrm((D_MODEL,), 0.02)

    w_in_even = nrm((N_EVEN, D_MODEL, EVEN_IN), D_MODEL ** -0.5)
    pool_w = nrm((N_EVEN, N_POOL_GROUPS, POOL_GROUP, POOL_GROUP), POOL_GROUP ** -0.5)
    pool_scale = 1.0 + nrm((N_EVEN, D_POOL), 0.02)
    conv_w = nrm((N_EVEN, CONV_K, D_CONV), CONV_K ** -0.5)
    w_out_even = nrm((N_EVEN, EVEN_MIX, D_MODEL), EVEN_MIX ** -0.5)

    w_in_odd = nrm((N_ODD, D_MODEL, ODD_IN), D_MODEL ** -0.5)
    ssm_conv_w = nrm((N_ODD, SSM_CONV_K, SSM_CONV_DIM), SSM_CONV_K ** -0.5)
    ssm_conv_b = nrm((N_ODD, SSM_CONV_DIM), 0.02)
    log_dt = jax.random.uniform(next(ks), (N_ODD, SSM_HEADS), f32, math.log(1e-3), math.log(1e-1))
    dt0 = jnp.exp(log_dt)
    dt_bias = dt0 + jnp.log(-jnp.expm1(-dt0))
    a_log = jnp.log(jax.random.uniform(next(ks), (N_ODD, SSM_HEADS), f32, 1.0, 16.0))
    d_skip = 1.0 + nrm((N_ODD, SSM_HEADS), 0.1)
    ssm_norm = 1.0 + nrm((N_ODD, D_INNER), 0.02)
    w_out_odd = nrm((N_ODD, D_INNER, D_MODEL), D_INNER ** -0.5)

    w_xa_q = nrm((DEPTH, D_MODEL, D_MODEL), D_MODEL ** -0.5)
    w_xa_k = nrm((DEPTH, D_MODEL, D_MODEL), D_MODEL ** -0.5)
    w_xa_v = nrm((DEPTH, D_MODEL, D_MODEL), D_MODEL ** -0.5)
    w_xa_o = nrm((DEPTH, D_MODEL, D_MODEL), D_MODEL ** -0.5)

    return {'x_prompt': x_prompt, 'x_sample': x_sample, 'mem_prompt': mem_prompt,
            'cache_mem_k': cache_mem_k, 'cache_mem_v': cache_mem_v, 'state_pool': state_pool,
            'state_conv': state_conv, 'state_ssm_conv': state_ssm_conv, 'state_ssm': state_ssm,
            'norm_mix': norm_mix, 'norm_xa': norm_xa, 'norm_final': norm_final,
            'w_in_even': w_in_even, 'pool_w': pool_w, 'pool_scale': pool_scale, 'conv_w': conv_w,
            'w_out_even': w_out_even, 'w_in_odd': w_in_odd, 'ssm_conv_w': ssm_conv_w,
            'ssm_conv_b': ssm_conv_b, 'dt_bias': dt_bias, 'a_log': a_log, 'd_skip': d_skip,
            'ssm_norm': ssm_norm, 'w_out_odd': w_out_odd, 'w_xa_q': w_xa_q, 'w_xa_k': w_xa_k,
            'w_xa_v': w_xa_v, 'w_xa_o': w_xa_o}


def reference(x_prompt, x_sample, mem_prompt, cache_mem_k, cache_mem_v, state_pool, state_conv,
              state_ssm_conv, state_ssm, norm_mix, norm_xa, norm_final, w_in_even, pool_w, pool_scale,
              conv_w, w_out_even, w_in_odd, ssm_conv_w, ssm_conv_b, dt_bias, a_log, d_skip, ssm_norm,
              w_out_odd, w_xa_q, w_xa_k, w_xa_v, w_xa_o):
    weights = (norm_mix, norm_xa, norm_final, w_in_even, pool_w, pool_scale, conv_w, w_out_even,
               w_in_odd, ssm_conv_w, ssm_conv_b, dt_bias, a_log, d_skip, ssm_norm, w_out_odd,
               w_xa_q, w_xa_o)
    bp = x_prompt.shape[0]
    dtype = x_prompt.dtype
    mem_k_prompt = jnp.einsum('bmd,lde->lbme', mem_prompt, w_xa_k).reshape(DEPTH, bp, N_MEM, XA_HEADS, XA_HEAD_DIM)
    mem_v_prompt = jnp.einsum('bmd,lde->lbme', mem_prompt, w_xa_v).reshape(DEPTH, bp, N_MEM, XA_HEADS, XA_HEAD_DIM)
    zero_pool = jnp.zeros((N_EVEN, bp, POOL_BUF, D_POOL), dtype)
    zero_conv = jnp.zeros((N_EVEN, bp, CONV_K - 1, D_CONV), dtype)
    zero_sconv = jnp.zeros((N_ODD, bp, SSM_CONV_K - 1, SSM_CONV_DIM), dtype)
    zero_ssm = jnp.zeros((N_ODD, bp, SSM_HEADS, SSM_HEAD_DIM, SSM_STATE), dtype)
    y_prompt, pool_prompt, conv_prompt, ssm_conv_prompt, ssm_prompt = run_trunk(
        x_prompt, mem_k_prompt, mem_v_prompt, zero_pool, zero_conv, zero_sconv, zero_ssm, 0, weights)
    y_sample, pool_sample, conv_sample, ssm_conv_sample, ssm_sample = run_trunk(
        x_sample, cache_mem_k, cache_mem_v, state_pool, state_conv, state_ssm_conv, state_ssm, PAST_LEN, weights)
    return (y_prompt, y_sample, mem_k_prompt, mem_v_prompt, pool_prompt, pool_sample, conv_prompt, conv_sample,
            ssm_conv_prompt, ssm_conv_sample, ssm_prompt, ssm_sample)
```

```python
import functools

import jax
import jax.numpy as jnp
from jax import lax
from jax.experimental import pallas as pl
from jax.experimental.pallas import tpu as pltpu

F32 = jnp.float32
BF16 = jnp.bfloat16

D_MODEL = 1024
PAST_LEN = 16384
POOL_WINDOWS = (2, 4, 8, 16)
POOL_GROUP = D_MODEL // len(POOL_WINDOWS)
POOL_BUF = max(POOL_WINDOWS) - 1
CONV_K = 3
D_INNER = 2 * D_MODEL
SSM_HEAD_DIM = 64
SSM_HEADS = D_INNER // SSM_HEAD_DIM
SSM_GROUPS = 4
SSM_STATE = 128
SSM_GROUP_W = D_INNER // SSM_GROUPS
SSM_CONV_K = 4
SSM_CONV_DIM = D_INNER + 2 * SSM_GROUPS * SSM_STATE
SSM_CHUNK = 128
N_MEM = 256
XA_HEADS = 4
XA_HEAD_DIM = D_MODEL // XA_HEADS
EPS = 1e-6
LANES = 128
HALO = 16
VMEM_LIMIT = 56 * 1024 * 1024


def _rmsnorm(x, g):
    return x * lax.rsqrt(jnp.mean(x * x, axis=-1, keepdims=True) + EPS) * g


def _silu(x):
    return x * jax.nn.sigmoid(x)


def _softplus(x):
    return jnp.maximum(x, 0.0) + jnp.log1p(jnp.exp(-jnp.abs(x)))


def _dot(a, b):
    return jnp.dot(a, b, preferred_element_type=F32)


def _dot_nt(a, b):
    return lax.dot_general(a, b, (((1,), (1,)), ((), ())), preferred_element_type=F32)


def _split3(x):
    hi = x.astype(BF16)
    r1 = x - hi.astype(F32)
    mid = r1.astype(BF16)
    lo = (r1 - mid.astype(F32)).astype(BF16)
    return hi, mid, lo


def _dot_exact_rhs(a_bf16, x):
    hi, mid, lo = _split3(x)
    return _dot(a_bf16, hi) + _dot(a_bf16, mid) + _dot(a_bf16, lo)


def _dot_exact_lhs(x, b_bf16):
    hi, mid, lo = _split3(x)
    return _dot(hi, b_bf16) + _dot(mid, b_bf16) + _dot(lo, b_bf16)


def _const_spec(shape):
    nd = len(shape)
    return pl.BlockSpec(shape, lambda *_: (0,) * nd, pipeline_mode=pl.Buffered(1))


def _params(sem):
    return pltpu.CompilerParams(dimension_semantics=sem, vmem_limit_bytes=VMEM_LIMIT)


def _kv_proj_kernel(mem_ref, wk_ref, wv_ref, k_ref, v_ref):
    m = mem_ref[...].astype(BF16)
    k_ref[0] = _dot(m, wk_ref[0].astype(BF16))
    v_ref[0] = _dot(m, wv_ref[0].astype(BF16))


def _kv_proj(mem2d, w_k, w_v):
    rows = mem2d.shape[0]
    depth = w_k.shape[0]
    tr = min(rows, 1024)
    out = jax.ShapeDtypeStruct((depth, rows, D_MODEL), F32)
    return pl.pallas_call(
        _kv_proj_kernel,
        grid=(depth, rows // tr),
        in_specs=[pl.BlockSpec((tr, D_MODEL), lambda l, r: (r, 0)),
                  pl.BlockSpec((1, D_MODEL, D_MODEL), lambda l, r: (l, 0, 0)),
                  pl.BlockSpec((1, D_MODEL, D_MODEL), lambda l, r: (l, 0, 0))],
        out_specs=[pl.BlockSpec((1, tr, D_MODEL), lambda l, r: (l, r, 0)),
                   pl.BlockSpec((1, tr, D_MODEL), lambda l, r: (l, r, 0))],
        out_shape=[out, out],
        compiler_params=_params(("arbitrary", "arbitrary")),
        name="kv_proj",
    )(mem2d, w_k, w_v)


def _even_prompt_kernel(x_ref, g_ref, w_in_ref, pw_ref, ps_ref, cw_ref, w_out_ref,
                        xo_ref, pool_ref, conv_ref, ext_ref, cext_ref, *, tile):
    t = pl.program_id(1)
    D = D_MODEL

    @pl.when(t == 0)
    def _():
        ext_ref[0:HALO, :] = jnp.zeros((HALO, D), F32)
        cext_ref[0:HALO, :] = jnp.zeros((HALO, D), F32)

    @pl.when(t > 0)
    def _():
        ext_ref[0:HALO, :] = ext_ref[tile:tile + HALO, :]
        cext_ref[0:HALO, :] = cext_ref[tile:tile + HALO, :]

    x = x_ref[0]
    h = _rmsnorm(x, g_ref[...]).astype(BF16)

    def proj(j):
        return _dot(h, w_in_ref[:, j * D:(j + 1) * D])

    u = proj(0)
    ext_ref[HALO:HALO + tile, :] = u
    g_pool = proj(1)
    pos = t * tile + lax.broadcasted_iota(jnp.int32, (tile, POOL_GROUP), 0)
    out = jnp.zeros((tile, D), F32)
    for g, w in enumerate(POOL_WINDOWS):
        sl = slice(g * POOL_GROUP, (g + 1) * POOL_GROUP)
        s = ext_ref[:, sl]
        k = 1
        while k < w:
            s = s + pltpu.roll(s, k, 0)
            k *= 2
        cnt = jnp.minimum(pos + 1, w).astype(F32)
        diff = s[HALO:, :] / cnt - u[:, sl]
        yp = _dot(diff.astype(BF16), pw_ref[g]) * ps_ref[:, sl] * _silu(g_pool[:, sl])
        out = out + _dot(yp.astype(BF16), w_out_ref[sl, :])

    b_gate = proj(2)
    cv = proj(3) * proj(4)
    cext_ref[HALO:HALO + tile, :] = cv
    ce = cext_ref[...]
    conv = cw_ref[2:3, :] * cv
    conv = conv + cw_ref[1:2, :] * pltpu.roll(ce, 1, 0)[HALO:, :]
    conv = conv + cw_ref[0:1, :] * pltpu.roll(ce, 2, 0)[HALO:, :]
    yc = b_gate * conv * _silu(proj(5))
    out = out + _dot(yc.astype(BF16), w_out_ref[D:2 * D, :])
    xo_ref[0] = x + out

    @pl.when(t == pl.num_programs(1) - 1)
    def _():
        pool_ref[0] = ext_ref[tile:tile + HALO, :]
        conv_ref[0] = cext_ref[tile:tile + HALO, :]


def _even_prompt(x, g, w_in, pw, ps, cw, w_out, tile):
    B, L, D = x.shape
    kern = functools.partial(_even_prompt_kernel, tile=tile)
    return pl.pallas_call(
        kern,
        grid=(B, L // tile),
        in_specs=[pl.BlockSpec((1, tile, D), lambda b, t: (b, t, 0)),
                  _const_spec(g.shape), _const_spec(w_in.shape), _const_spec(pw.shape),
                  _const_spec(ps.shape), _const_spec(cw.shape), _const_spec(w_out.shape)],
        out_specs=[pl.BlockSpec((1, tile, D), lambda b, t: (b, t, 0)),
                   pl.BlockSpec((1, HALO, D), lambda b, t: (b, 0, 0)),
                   pl.BlockSpec((1, HALO, D), lambda b, t: (b, 0, 0))],
        out_shape=[jax.ShapeDtypeStruct((B, L, D), F32),
                   jax.ShapeDtypeStruct((B, HALO, D), F32),
                   jax.ShapeDtypeStruct((B, HALO, D), F32)],
        scratch_shapes=[pltpu.VMEM((HALO + tile, D), F32), pltpu.VMEM((HALO + tile, D), F32)],
        compiler_params=_params(("arbitrary", "arbitrary")),
        name="even_prompt",
    )(x, g, w_in, pw, ps, cw, w_out)


def _xattn_prompt_kernel(x_ref, g_ref, wq_ref, wo_ref, k_ref, v_ref, gf_ref, xo_ref, *, final):
    x = x_ref[0]
    h = _rmsnorm(x, g_ref[...]).astype(BF16)
    q = (_dot(h, wq_ref[...]) * (XA_HEAD_DIM ** -0.5)).astype(BF16)
    out = jnp.zeros_like(x)
    for hd in range(XA_HEADS):
        sl = slice(hd * XA_HEAD_DIM, (hd + 1) * XA_HEAD_DIM)
        kh = k_ref[0, :, sl].astype(BF16)
        vh = v_ref[0, :, sl].astype(BF16)
        s = _dot_nt(q[:, sl], kh)
        e = jnp.exp(s - jnp.max(s, axis=-1, keepdims=True))
        o = _dot(e.astype(BF16), vh) / jnp.sum(e, axis=-1, keepdims=True)
        out = out + _dot(o.astype(BF16), wo_ref[sl, :])
    y = x + out
    if final:
        y = _rmsnorm(y, gf_ref[...])
    xo_ref[0] = y


def _xattn_prompt(x, g, wq, wo, k, v, gf, tile, final):
    B, L, D = x.shape
    kern = functools.partial(_xattn_prompt_kernel, final=final)
    return pl.pallas_call(
        kern,
        grid=(B, L // tile),
        in_specs=[pl.BlockSpec((1, tile, D), lambda b, t: (b, t, 0)),
                  _const_spec(g.shape), _const_spec(wq.shape), _const_spec(wo.shape),
                  pl.BlockSpec((1, N_MEM, D), lambda b, t: (b, 0, 0)),
                  pl.BlockSpec((1, N_MEM, D), lambda b, t: (b, 0, 0)),
                  _const_spec(gf.shape)],
        out_specs=pl.BlockSpec((1, tile, D), lambda b, t: (b, t, 0)),
        out_shape=jax.ShapeDtypeStruct((B, L, D), F32),
        compiler_params=_params(("arbitrary", "arbitrary")),
        name="xattn_prompt",
    )(x, g, wq, wo, k, v, gf)


def _odd_prompt_kernel(x_ref, g_ref, wz_ref, wxbc_ref, wdt_ref, cw_ref, cb_ref, dtb_ref, alog_ref,
                       dskip_ref, nw_ref, w_out_ref,
                       xo_ref, sconv_ref, state_ref, xext_ref, st_ref, y_ref, *, tile):
    t = pl.program_id(1)
    Q = SSM_CHUNK
    C = SSM_CONV_DIM

    @pl.when(t == 0)
    def _():
        xext_ref[0:HALO, :] = jnp.zeros((HALO, C), F32)
        st_ref[...] = jnp.zeros(st_ref.shape, F32)

    @pl.when(t > 0)
    def _():
        xext_ref[0:HALO, :] = xext_ref[tile:tile + HALO, :]

    x = x_ref[0]
    h = _rmsnorm(x, g_ref[...]).astype(BF16)
    xbc = _dot(h, wxbc_ref[...])
    xext_ref[HALO:HALO + tile, :] = xbc
    xe = xext_ref[...]
    conv = cw_ref[3:4, :] * xbc
    for k in range(1, SSM_CONV_K):
        conv = conv + cw_ref[3 - k:4 - k, :] * pltpu.roll(xe, k, 0)[HALO:, :]
    xc = _silu(conv + cb_ref[...])
    dt = _softplus(_dot(h, wdt_ref[...]) + dtb_ref[...])
    dta = dt * (-jnp.exp(alog_ref[...]))

    row = lax.broadcasted_iota(jnp.int32, (Q, Q), 0)
    col = lax.broadcasted_iota(jnp.int32, (Q, Q), 1)
    causal = row >= col
    tri = jnp.where(causal, 1.0, 0.0).astype(BF16)
    lo_lane = lax.broadcasted_iota(jnp.int32, (1, LANES), 1) < SSM_HEAD_DIM
    neg_inf = jnp.float32(-jnp.inf)

    for c in range(tile // Q):
        rs = slice(c * Q, (c + 1) * Q)
        a_cs = _dot_exact_rhs(tri, dta[rs, :])
        a_cs_t = a_cs.T
        dt_t = dt[rs, :].T
        w_t = dt_t * jnp.exp(a_cs_t[:, Q - 1:Q] - a_cs_t)
        ea = jnp.exp(a_cs)
        dec = ea[Q - 1:Q, :]
        for g in range(SSM_GROUPS):
            bsl = slice(D_INNER + g * SSM_STATE, D_INNER + (g + 1) * SSM_STATE)
            csl = slice(D_INNER + SSM_GROUPS * SSM_STATE + g * SSM_STATE,
                        D_INNER + SSM_GROUPS * SSM_STATE + (g + 1) * SSM_STATE)
            b_t = xc[rs, bsl].T
            c_g = xc[rs, csl].astype(BF16)
            cb = _dot(c_g, b_t.astype(BF16))
            gsl = slice(g * SSM_GROUP_W, (g + 1) * SSM_GROUP_W)
            y_off = _dot(c_g, st_ref[:, gsl].astype(BF16))
            for j in range(SSM_GROUP_W // LANES):
                h0 = (g * SSM_GROUP_W + j * LANES) // SSM_HEAD_DIM
                lsl = slice(g * SSM_GROUP_W + j * LANES, g * SSM_GROUP_W + (j + 1) * LANES)
                xs = xc[rs, lsl]
                wmat = jnp.concatenate([jnp.where(lo_lane, xs, 0.0), jnp.where(lo_lane, 0.0, xs)],
                                       axis=0).astype(BF16)
                tops, bots = [], []
                for hh in (h0, h0 + 1):
                    seg = a_cs[:, hh:hh + 1] - a_cs_t[hh:hh + 1, :]
                    decay = jnp.exp(jnp.where(causal, seg, neg_inf))
                    tops.append(cb * decay * dt_t[hh:hh + 1, :])
                    bots.append(b_t * w_t[hh:hh + 1, :])
                lhs = jnp.concatenate([jnp.concatenate(tops, axis=1), jnp.concatenate(bots, axis=1)],
                                      axis=0).astype(BF16)
                r = _dot(lhs, wmat)
                escale = jnp.where(lo_lane, ea[:, h0:h0 + 1], ea[:, h0 + 1:h0 + 2])
                y_ref[rs, lsl] = (r[0:Q, :] + y_off[:, j * LANES:(j + 1) * LANES] * escale
                                  + dskip_ref[:, lsl] * xs)
                dscale = jnp.where(lo_lane, dec[:, h0:h0 + 1], dec[:, h0 + 1:h0 + 2])
                st_ref[:, lsl] = st_ref[:, lsl] * dscale + r[Q:2 * Q, :]

    yg = y_ref[...] * _silu(_dot(h, wz_ref[...]))
    out = jnp.zeros((tile, D_MODEL), F32)
    for g in range(SSM_GROUPS):
        gsl = slice(g * SSM_GROUP_W, (g + 1) * SSM_GROUP_W)
        ygg = yg[:, gsl]
        yn = ygg * lax.rsqrt(jnp.mean(ygg * ygg, axis=-1, keepdims=True) + EPS) * nw_ref[:, gsl]
        out = out + _dot(yn.astype(BF16), w_out_ref[gsl, :])
    xo_ref[0] = x + out

    @pl.when(t == pl.num_programs(1) - 1)
    def _():
        sconv_ref[0] = xext_ref[tile:tile + HALO, :]
        state_ref[0] = st_ref[...].T


def _odd_prompt(x, g, wz, wxbc, wdt, cw, cb, dtb, alog, dskip, nw, w_out, tile):
    B, L, D = x.shape
    kern = functools.partial(_odd_prompt_kernel, tile=tile)
    consts = (g, wz, wxbc, wdt, cw, cb, dtb, alog, dskip, nw, w_out)
    return pl.pallas_call(
        kern,
        grid=(B, L // tile),
        in_specs=[pl.BlockSpec((1, tile, D), lambda b, t: (b, t, 0))] + [_const_spec(a.shape) for a in consts],
        out_specs=[pl.BlockSpec((1, tile, D), lambda b, t: (b, t, 0)),
                   pl.BlockSpec((1, HALO, SSM_CONV_DIM), lambda b, t: (b, 0, 0)),
                   pl.BlockSpec((1, D_INNER, SSM_STATE), lambda b, t: (b, 0, 0))],
        out_shape=[jax.ShapeDtypeStruct((B, L, D), F32),
                   jax.ShapeDtypeStruct((B, HALO, SSM_CONV_DIM), F32),
                   jax.ShapeDtypeStruct((B, D_INNER, SSM_STATE), F32)],
        scratch_shapes=[pltpu.VMEM((HALO + tile, SSM_CONV_DIM), F32),
                        pltpu.VMEM((SSM_STATE, D_INNER), F32),
                        pltpu.VMEM((tile, D_INNER), F32)],
        compiler_params=_params(("arbitrary", "arbitrary")),
        name="odd_prompt",
    )(x, *consts)


def _even_sample_kernel(x_ref, g_ref, w_in_ref, pw_ref, ps_ref, cw_ref, w_out_ref, pool_ref, cst_ref,
                        xo_ref, pool_o_ref, cst_o_ref):
    D = D_MODEL
    x = x_ref[...]
    h = _rmsnorm(x, g_ref[...]).astype(BF16)

    def proj(j):
        return _dot(h, w_in_ref[:, j * D:(j + 1) * D])

    u = proj(0)
    g_pool = proj(1)
    out = jnp.zeros_like(x)
    for g, w in enumerate(POOL_WINDOWS):
        sl = slice(g * POOL_GROUP, (g + 1) * POOL_GROUP)
        s = u[:, sl]
        for j in range(POOL_BUF - (w - 1), POOL_BUF):
            s = s + pool_ref[j, :, sl]
        cnt = float(min(PAST_LEN + 1, w))
        diff = s / cnt - u[:, sl]
        yp = _dot(diff.astype(BF16), pw_ref[g]) * ps_ref[:, sl] * _silu(g_pool[:, sl])
        out = out + _dot(yp.astype(BF16), w_out_ref[sl, :])
    for j in range(POOL_BUF - 1):
        pool_o_ref[j] = pool_ref[j + 1]
    pool_o_ref[POOL_BUF - 1] = u

    b_gate = proj(2)
    cv = proj(3) * proj(4)
    conv = cw_ref[0:1, :] * cst_ref[0] + cw_ref[1:2, :] * cst_ref[1] + cw_ref[2:3, :] * cv
    cst_o_ref[0] = cst_ref[1]
    cst_o_ref[1] = cv
    yc = b_gate * conv * _silu(proj(5))
    out = out + _dot(yc.astype(BF16), w_out_ref[D:2 * D, :])
    xo_ref[...] = x + out


def _even_sample(x, g, w_in, pw, ps, cw, w_out, pool_t, cst_t):
    args = (x, g, w_in, pw, ps, cw, w_out, pool_t, cst_t)
    outs = (x, pool_t, cst_t)
    return pl.pallas_call(
        _even_sample_kernel,
        grid=(1,),
        in_specs=[_const_spec(a.shape) for a in args],
        out_specs=[_const_spec(a.shape) for a in outs],
        out_shape=[jax.ShapeDtypeStruct(a.shape, F32) for a in outs],
        compiler_params=_params(("arbitrary",)),
        name="even_sample",
    )(*args)


def _xattn_sample_kernel(x_ref, g_ref, wq_ref, wo_ref, k_ref, v_ref, gf_ref, xo_ref, q_ref, o_ref,
                         *, bt, final):
    i = pl.program_id(0)

    @pl.when(i == 0)
    def _():
        h = _rmsnorm(x_ref[...], g_ref[...]).astype(BF16)
        q_ref[...] = _dot(h, wq_ref[...]) * (XA_HEAD_DIM ** -0.5)

    base = pl.multiple_of(i * bt, bt)
    for b in range(bt):
        qb = q_ref[pl.ds(base + b, 1), :]
        prod = k_ref[b] * qb
        vb = v_ref[b]
        parts = []
        for hd in range(XA_HEADS):
            sl = slice(hd * XA_HEAD_DIM, (hd + 1) * XA_HEAD_DIM)
            s = jnp.sum(prod[:, sl], axis=-1, keepdims=True)
            e = jnp.exp(s - jnp.max(s, axis=0, keepdims=True))
            p = e / jnp.sum(e, axis=0, keepdims=True)
            parts.append(jnp.sum(p * vb[:, sl], axis=0, keepdims=True))
        o_ref[pl.ds(base + b, 1), :] = jnp.concatenate(parts, axis=1)

    @pl.when(i == pl.num_programs(0) - 1)
    def _():
        y = x_ref[...] + _dot(o_ref[...].astype(BF16), wo_ref[...])
        if final:
            y = _rmsnorm(y, gf_ref[...])
        xo_ref[...] = y


def _xattn_sample(x, g, wq, wo, k, v, gf, bt, final):
    DB, D = x.shape
    kern = functools.partial(_xattn_sample_kernel, bt=bt, final=final)
    return pl.pallas_call(
        kern,
        grid=(DB // bt,),
        in_specs=[_const_spec(x.shape), _const_spec(g.shape), _const_spec(wq.shape), _const_spec(wo.shape),
                  pl.BlockSpec((bt, N_MEM, D), lambda i: (i, 0, 0)),
                  pl.BlockSpec((bt, N_MEM, D), lambda i: (i, 0, 0)),
                  _const_spec(gf.shape)],
        out_specs=pl.BlockSpec((DB, D), lambda i: (0, 0)),
        out_shape=jax.ShapeDtypeStruct((DB, D), F32),
        scratch_shapes=[pltpu.VMEM((DB, D), F32), pltpu.VMEM((DB, D), F32)],
        compiler_params=_params(("arbitrary",)),
        name="xattn_sample",
    )(x, g, wq, wo, k, v, gf)


def _odd_sample_in_kernel(x_ref, g_ref, wz_ref, wxbc_ref, wdt_ref, cw_ref, cb_ref, dtb_ref, alog_ref, sconv_ref,
                          z_ref, xs_ref, bm_ref, cm_ref, xdt_ref, dec_ref, sconv_o_ref):
    h = _rmsnorm(x_ref[...], g_ref[...]).astype(BF16)
    xbc = _dot(h, wxbc_ref[...])
    conv = cw_ref[3:4, :] * xbc
    for k in range(SSM_CONV_K - 1):
        conv = conv + cw_ref[k:k + 1, :] * sconv_ref[k]
    for k in range(SSM_CONV_K - 2):
        sconv_o_ref[k] = sconv_ref[k + 1]
    sconv_o_ref[SSM_CONV_K - 2] = xbc
    xc = _silu(conv + cb_ref[...])
    xs = xc[:, :D_INNER]
    dt = _softplus(_dot(h, wdt_ref[...]) + dtb_ref[...])
    dta = dt * (-jnp.exp(alog_ref[...]))
    hrow = lax.broadcasted_iota(jnp.int32, (LANES, D_INNER), 0)
    ccol = lax.broadcasted_iota(jnp.int32, (LANES, D_INNER), 1)
    expand = jnp.where(ccol // SSM_HEAD_DIM == hrow, 1.0, 0.0).astype(BF16)
    z_ref[...] = _dot(h, wz_ref[...])
    xs_ref[...] = xs
    bm_ref[...] = xc[:, D_INNER:D_INNER + SSM_GROUPS * SSM_STATE]
    cm_ref[...] = xc[:, D_INNER + SSM_GROUPS * SSM_STATE:]
    xdt_ref[...] = xs * _dot_exact_lhs(dt, expand)
    dec_ref[...] = jnp.exp(_dot_exact_lhs(dta, expand))


def _odd_sample_in(x, g, wz, wxbc, wdt, cw, cb, dtb, alog, sconv_t):
    DB = x.shape[0]
    args = (x, g, wz, wxbc, wdt, cw, cb, dtb, alog, sconv_t)
    shapes = [(DB, D_INNER), (DB, D_INNER), (DB, SSM_GROUPS * SSM_STATE), (DB, SSM_GROUPS * SSM_STATE),
              (DB, D_INNER), (DB, D_INNER), sconv_t.shape]
    return pl.pallas_call(
        _odd_sample_in_kernel,
        grid=(1,),
        in_specs=[_const_spec(a.shape) for a in args],
        out_specs=[_const_spec(s) for s in shapes],
        out_shape=[jax.ShapeDtypeStruct(s, F32) for s in shapes],
        compiler_params=_params(("arbitrary",)),
        name="odd_sample_in",
    )(*args)


def _odd_sample_state_kernel(xdt_ref, dec_ref, bm_ref, cm_ref, s_ref, so_ref, y_ref, *, bt):
    pad = jnp.zeros((LANES - bt, D_INNER), F32)
    xdt_c = jnp.concatenate([xdt_ref[...], pad], axis=0).T
    dec_c = jnp.concatenate([dec_ref[...], pad], axis=0).T
    lane = lax.broadcasted_iota(jnp.int32, (1, LANES), 1)
    y_c = jnp.zeros((D_INNER, LANES), F32)
    for b in range(bt):
        ys = []
        for g in range(SSM_GROUPS):
            gsl = slice(g * SSM_GROUP_W, (g + 1) * SSM_GROUP_W)
            nsl = slice(g * SSM_STATE, (g + 1) * SSM_STATE)
            new = s_ref[b, gsl, :] * dec_c[gsl, b:b + 1] + xdt_c[gsl, b:b + 1] * bm_ref[b:b + 1, nsl]
            so_ref[b, gsl, :] = new
            ys.append(jnp.sum(new * cm_ref[b:b + 1, nsl], axis=-1, keepdims=True))
        y_c = y_c + jnp.where(lane == b, jnp.concatenate(ys, axis=0), 0.0)
    y_ref[...] = y_c.T[0:bt, :]


def _odd_sample_state(xdt, dec, bm, cm, state, bt):
    DB = xdt.shape[0]
    kern = functools.partial(_odd_sample_state_kernel, bt=bt)
    gn = SSM_GROUPS * SSM_STATE
    return pl.pallas_call(
        kern,
        grid=(DB // bt,),
        in_specs=[pl.BlockSpec((bt, D_INNER), lambda i: (i, 0)),
                  pl.BlockSpec((bt, D_INNER), lambda i: (i, 0)),
                  pl.BlockSpec((bt, gn), lambda i: (i, 0)),
                  pl.BlockSpec((bt, gn), lambda i: (i, 0)),
                  pl.BlockSpec((bt, D_INNER, SSM_STATE), lambda i: (i, 0, 0))],
        out_specs=[pl.BlockSpec((bt, D_INNER, SSM_STATE), lambda i: (i, 0, 0)),
                   pl.BlockSpec((bt, D_INNER), lambda i: (i, 0))],
        out_shape=[jax.ShapeDtypeStruct(state.shape, F32), jax.ShapeDtypeStruct((DB, D_INNER), F32)],
        compiler_params=_params(("arbitrary",)),
        name="odd_sample_state",
    )(xdt, dec, bm, cm, state)


def _odd_sample_out_kernel(x_ref, y_ref, xs_ref, z_ref, dskip_ref, nw_ref, w_out_ref, xo_ref):
    yg = (y_ref[...] + dskip_ref[...] * xs_ref[...]) * _silu(z_ref[...])
    out = jnp.zeros(x_ref.shape, F32)
    for g in range(SSM_GROUPS):
        gsl = slice(g * SSM_GROUP_W, (g + 1) * SSM_GROUP_W)
        ygg = yg[:, gsl]
        yn = ygg * lax.rsqrt(jnp.mean(ygg * ygg, axis=-1, keepdims=True) + EPS) * nw_ref[:, gsl]
        out = out + _dot(yn.astype(BF16), w_out_ref[gsl, :])
    xo_ref[...] = x_ref[...] + out


def _odd_sample_out(x, y, xs, z, dskip, nw, w_out):
    args = (x, y, xs, z, dskip, nw, w_out)
    return pl.pallas_call(
        _odd_sample_out_kernel,
        grid=(1,),
        in_specs=[_const_spec(a.shape) for a in args],
        out_specs=_const_spec(x.shape),
        out_shape=jax.ShapeDtypeStruct(x.shape, F32),
        compiler_params=_params(("arbitrary",)),
        name="odd_sample_out",
    )(*args)


def _pad_lanes(v):
    return jnp.pad(v, ((0, 0), (0, LANES - v.shape[-1])))


def kernel(x_prompt, x_sample, mem_prompt, cache_mem_k, cache_mem_v, state_pool, state_conv, state_ssm_conv,
           state_ssm, norm_mix, norm_xa, norm_final, w_in_even, pool_w, pool_scale, conv_w, w_out_even, w_in_odd,
           ssm_conv_w, ssm_conv_b, dt_bias, a_log, d_skip, ssm_norm, w_out_odd, w_xa_q, w_xa_k, w_xa_v, w_xa_o):
    B, L, D = x_prompt.shape
    DB = x_sample.shape[0]
    depth = norm_mix.shape[0]
    tile = min(L, 256)
    bt_x = min(DB, 8)
    bt_s = min(DB, 8)

    mem_k, mem_v = _kv_proj(mem_prompt.reshape(B * N_MEM, D), w_xa_k, w_xa_v)
    mem_k = mem_k.reshape(depth, B, N_MEM, D)
    mem_v = mem_v.reshape(depth, B, N_MEM, D)
    cache_k = cache_mem_k.reshape(depth, DB, N_MEM, D)
    cache_v = cache_mem_v.reshape(depth, DB, N_MEM, D)

    xp = x_prompt
    xs_ = x_sample.reshape(DB, D)
    gf = norm_final.reshape(1, D)
    pool_p, pool_s, conv_p, conv_s, sconv_p, sconv_s, ssm_p, ssm_s = [], [], [], [], [], [], [], []
    for l in range(depth):
        i = l // 2
        g_mix = norm_mix[l].reshape(1, D)
        if l % 2 == 0:
            w_in = w_in_even[i].astype(BF16)
            pw = pool_w[i].astype(BF16)
            ps = pool_scale[i].reshape(1, D)
            cw = conv_w[i]
            w_out = w_out_even[i].astype(BF16)
            xp, pb, cbuf = _even_prompt(xp, g_mix, w_in, pw, ps, cw, w_out, tile)
            pool_p.append(pb[:, HALO - POOL_BUF:, :])
            conv_p.append(cbuf[:, HALO - (CONV_K - 1):, :])
            xs_, pool_t, cst_t = _even_sample(xs_, g_mix, w_in, pw, ps, cw, w_out,
                                              state_pool[i].transpose(1, 0, 2), state_conv[i].transpose(1, 0, 2))
            pool_s.append(pool_t.transpose(1, 0, 2))
            conv_s.append(cst_t.transpose(1, 0, 2))
        else:
            wz = w_in_odd[i][:, :D_INNER].astype(BF16)
            wxbc = w_in_odd[i][:, D_INNER:D_INNER + SSM_CONV_DIM].astype(BF16)
            wdt = _pad_lanes(w_in_odd[i][:, D_INNER + SSM_CONV_DIM:]).astype(BF16)
            cw = ssm_conv_w[i]
            cb = ssm_conv_b[i].reshape(1, -1)
            dtb = _pad_lanes(dt_bias[i].reshape(1, -1))
            alog = _pad_lanes(a_log[i].reshape(1, -1))
            dskip = jnp.repeat(d_skip[i], SSM_HEAD_DIM).reshape(1, D_INNER)
            nw = ssm_norm[i].reshape(1, D_INNER)
            w_out = w_out_odd[i].astype(BF16)
            xp, sb, st = _odd_prompt(xp, g_mix, wz, wxbc, wdt, cw, cb, dtb, alog, dskip, nw, w_out, tile)
            sconv_p.append(sb[:, HALO - (SSM_CONV_K - 1):, :])
            ssm_p.append(st.reshape(B, SSM_HEADS, SSM_HEAD_DIM, SSM_STATE))
            z, xs_in, bm, cm, xdt, dec, sconv_t = _odd_sample_in(
                xs_, g_mix, wz, wxbc, wdt, cw, cb, dtb, alog, state_ssm_conv[i].transpose(1, 0, 2))
            st_new, y = _odd_sample_state(xdt, dec, bm, cm, state_ssm[i].reshape(DB, D_INNER, SSM_STATE), bt_s)
            xs_ = _odd_sample_out(xs_, y, xs_in, z, dskip, nw, w_out)
            sconv_s.append(sconv_t.transpose(1, 0, 2))
            ssm_s.append(st_new.reshape(DB, SSM_HEADS, SSM_HEAD_DIM, SSM_STATE))
        g_xa = norm_xa[l].reshape(1, D)
        wq = w_xa_q[l].astype(BF16)
        wo = w_xa_o[l].astype(BF16)
        final = l == depth - 1
        xp = _xattn_prompt(xp, g_xa, wq, wo, mem_k[l], mem_v[l], gf, tile, final)
        xs_ = _xattn_sample(xs_, g_xa, wq, wo, cache_k[l], cache_v[l], gf, bt_x, final)

    shape5 = (depth, B, N_MEM, XA_HEADS, XA_HEAD_DIM)
    return (xp, xs_.reshape(DB, 1, D), mem_k.reshape(shape5), mem_v.reshape(shape5),
            jnp.stack(pool_p), jnp.stack(pool_s), jnp.stack(conv_p), jnp.stack(conv_s),
            jnp.stack(sconv_p), jnp.stack(sconv_s), jnp.stack(ssm_p), jnp.stack(ssm_s))
```

```python
import functools

import jax
import jax.numpy as jnp
from jax import lax
from jax.experimental import pallas as pl
from jax.experimental.pallas import tpu as pltpu

F32 = jnp.float32
BF16 = jnp.bfloat16

D_MODEL = 1024
PAST_LEN = 16384
POOL_WINDOWS = (2, 4, 8, 16)
POOL_GROUP = D_MODEL // len(POOL_WINDOWS)
POOL_BUF = max(POOL_WINDOWS) - 1
CONV_K = 3
D_INNER = 2 * D_MODEL
SSM_HEAD_DIM = 64
SSM_HEADS = D_INNER // SSM_HEAD_DIM
SSM_GROUPS = 4
SSM_STATE = 128
SSM_GROUP_W = D_INNER // SSM_GROUPS
SSM_CONV_K = 4
SSM_CONV_DIM = D_INNER + 2 * SSM_GROUPS * SSM_STATE
SSM_CHUNK = 128
N_MEM = 256
XA_HEADS = 4
XA_HEAD_DIM = D_MODEL // XA_HEADS
EPS = 1e-6
LANES = 128
HALO = 16
VMEM_LIMIT = 56 * 1024 * 1024


def _rmsnorm(x, g):
    return x * lax.rsqrt(jnp.mean(x * x, axis=-1, keepdims=True) + EPS) * g


def _silu(x):
    return x * jax.nn.sigmoid(x)


def _softplus(x):
    return jnp.maximum(x, 0.0) + jnp.log1p(jnp.exp(-jnp.abs(x)))


def _dot(a, b):
    return jnp.dot(a, b, preferred_element_type=F32)


def _dot_nt(a, b):
    return lax.dot_general(a, b, (((1,), (1,)), ((), ())), preferred_element_type=F32)


def _split3(x):
    hi = x.astype(BF16)
    r1 = x - hi.astype(F32)
    mid = r1.astype(BF16)
    lo = (r1 - mid.astype(F32)).astype(BF16)
    return hi, mid, lo


def _dot_exact_rhs(a_bf16, x):
    hi, mid, lo = _split3(x)
    return _dot(a_bf16, hi) + _dot(a_bf16, mid) + _dot(a_bf16, lo)


def _dot_exact_lhs(x, b_bf16):
    hi, mid, lo = _split3(x)
    return _dot(hi, b_bf16) + _dot(mid, b_bf16) + _dot(lo, b_bf16)


def _const_spec(shape):
    nd = len(shape)
    return pl.BlockSpec(shape, lambda *_: (0,) * nd, pipeline_mode=pl.Buffered(1))


def _layer(arr, l):
    return (arr, l)


def _operands(ops):
    arrays, specs = [], []
    for op in ops:
        if isinstance(op, tuple):
            arr, l = op
            nd = arr.ndim - 1
            specs.append(pl.BlockSpec((None,) + arr.shape[1:], lambda *_, l=l, nd=nd: (l,) + (0,) * nd,
                                      pipeline_mode=pl.Buffered(1)))
        else:
            arr = op
            specs.append(_const_spec(arr.shape))
        arrays.append(arr)
    return arrays, specs


def _params(sem):
    return pltpu.CompilerParams(dimension_semantics=sem, vmem_limit_bytes=VMEM_LIMIT)


def _head_rows_order():
    lane_tiles = XA_HEAD_DIM // LANES
    return [(r % XA_HEADS) * lane_tiles + r // XA_HEADS for r in range(XA_HEADS * lane_tiles)]


def _to_head_slabs(a):
    lead = a.shape[:-2]
    lane_tiles = XA_HEAD_DIM // LANES
    nd = len(lead)
    a = a.reshape(lead + (XA_HEADS, lane_tiles, LANES))
    a = a.transpose(tuple(range(nd)) + (nd + 1, nd, nd + 2))
    return a.reshape(lead + (XA_HEADS * lane_tiles, LANES))


def _from_head_slabs(a):
    lead = a.shape[:-2]
    lane_tiles = XA_HEAD_DIM // LANES
    nd = len(lead)
    a = a.reshape(lead + (lane_tiles, XA_HEADS, LANES))
    a = a.transpose(tuple(range(nd)) + (nd + 1, nd, nd + 2))
    return a.reshape(lead + (XA_HEADS, XA_HEAD_DIM))


def _kv_proj_kernel(mem_ref, wk_ref, wv_ref, k_ref, v_ref, k8_ref, v8_ref):
    m = mem_ref[...].astype(BF16)
    k = _dot(m, wk_ref[0].astype(BF16))
    v = _dot(m, wv_ref[0].astype(BF16))
    k_ref[0] = k
    v_ref[0] = v
    for r, j in enumerate(_head_rows_order()):
        k8_ref[0, :, r, :] = k[:, j * LANES:(j + 1) * LANES]
        v8_ref[0, :, r, :] = v[:, j * LANES:(j + 1) * LANES]


def _kv_proj(mem2d, w_k, w_v):
    rows = mem2d.shape[0]
    depth = w_k.shape[0]
    tr = min(rows, 512)
    out = jax.ShapeDtypeStruct((depth, rows, D_MODEL), F32)
    out8 = jax.ShapeDtypeStruct((depth, rows, D_MODEL // LANES, LANES), F32)
    return pl.pallas_call(
        _kv_proj_kernel,
        grid=(depth, rows // tr),
        in_specs=[pl.BlockSpec((tr, D_MODEL), lambda l, r: (r, 0)),
                  pl.BlockSpec((1, D_MODEL, D_MODEL), lambda l, r: (l, 0, 0)),
                  pl.BlockSpec((1, D_MODEL, D_MODEL), lambda l, r: (l, 0, 0))],
        out_specs=[pl.BlockSpec((1, tr, D_MODEL), lambda l, r: (l, r, 0)),
                   pl.BlockSpec((1, tr, D_MODEL), lambda l, r: (l, r, 0)),
                   pl.BlockSpec((1, tr, D_MODEL // LANES, LANES), lambda l, r: (l, r, 0, 0)),
                   pl.BlockSpec((1, tr, D_MODEL // LANES, LANES), lambda l, r: (l, r, 0, 0))],
        out_shape=[out, out, out8, out8],
        compiler_params=_params(("arbitrary", "arbitrary")),
        name="kv_proj",
    )(mem2d, w_k, w_v)


def _even_prompt_kernel(x_ref, g_ref, w_in_ref, pw_ref, ps_ref, cw_ref, w_out_ref,
                        xo_ref, pool_ref, conv_ref, ext_ref, cext_ref, *, tile):
    t = pl.program_id(1)
    D = D_MODEL

    @pl.when(t == 0)
    def _():
        ext_ref[0:HALO, :] = jnp.zeros((HALO, D), F32)
        cext_ref[0:HALO, :] = jnp.zeros((HALO, D), F32)

    @pl.when(t > 0)
    def _():
        ext_ref[0:HALO, :] = ext_ref[tile:tile + HALO, :]
        cext_ref[0:HALO, :] = cext_ref[tile:tile + HALO, :]

    x = x_ref[0]
    h = _rmsnorm(x, g_ref[...]).astype(BF16)

    def proj(j):
        return _dot(h, w_in_ref[:, j * D:(j + 1) * D])

    u = proj(0)
    ext_ref[HALO:HALO + tile, :] = u
    g_pool = proj(1)
    pos = t * tile + lax.broadcasted_iota(jnp.int32, (tile, POOL_GROUP), 0)
    out = jnp.zeros((tile, D), F32)
    for g, w in enumerate(POOL_WINDOWS):
        sl = slice(g * POOL_GROUP, (g + 1) * POOL_GROUP)
        s = ext_ref[:, sl]
        k = 1
        while k < w:
            s = s + pltpu.roll(s, k, 0)
            k *= 2
        cnt = jnp.minimum(pos + 1, w).astype(F32)
        diff = s[HALO:, :] / cnt - u[:, sl]
        yp = _dot(diff.astype(BF16), pw_ref[g]) * ps_ref[:, sl] * _silu(g_pool[:, sl])
        out = out + _dot(yp.astype(BF16), w_out_ref[sl, :])

    b_gate = proj(2)
    cv = proj(3) * proj(4)
    cext_ref[HALO:HALO + tile, :] = cv
    ce = cext_ref[...]
    conv = cw_ref[2:3, :] * cv
    conv = conv + cw_ref[1:2, :] * pltpu.roll(ce, 1, 0)[HALO:, :]
    conv = conv + cw_ref[0:1, :] * pltpu.roll(ce, 2, 0)[HALO:, :]
    yc = b_gate * conv * _silu(proj(5))
    out = out + _dot(yc.astype(BF16), w_out_ref[D:2 * D, :])
    xo_ref[0] = x + out

    @pl.when(t == pl.num_programs(1) - 1)
    def _():
        pool_ref[0] = ext_ref[tile:tile + HALO, :]
        conv_ref[0] = cext_ref[tile:tile + HALO, :]


def _even_prompt(x, consts, tile):
    B, L, D = x.shape
    kern = functools.partial(_even_prompt_kernel, tile=tile)
    arrays, specs = _operands(consts)
    return pl.pallas_call(
        kern,
        grid=(B, L // tile),
        in_specs=[pl.BlockSpec((1, tile, D), lambda b, t: (b, t, 0))] + specs,
        out_specs=[pl.BlockSpec((1, tile, D), lambda b, t: (b, t, 0)),
                   pl.BlockSpec((1, HALO, D), lambda b, t: (b, 0, 0)),
                   pl.BlockSpec((1, HALO, D), lambda b, t: (b, 0, 0))],
        out_shape=[jax.ShapeDtypeStruct((B, L, D), F32),
                   jax.ShapeDtypeStruct((B, HALO, D), F32),
                   jax.ShapeDtypeStruct((B, HALO, D), F32)],
        scratch_shapes=[pltpu.VMEM((HALO + tile, D), F32), pltpu.VMEM((HALO + tile, D), F32)],
        compiler_params=_params(("arbitrary", "arbitrary")),
        name="even_prompt",
    )(x, *arrays)


def _xattn_prompt_kernel(x_ref, g_ref, wq_ref, wo_ref, k_ref, v_ref, gf_ref, xo_ref, *, final):
    x = x_ref[0]
    h = _rmsnorm(x, g_ref[...]).astype(BF16)
    q = (_dot(h, wq_ref[...]) * (XA_HEAD_DIM ** -0.5)).astype(BF16)
    out = jnp.zeros_like(x)
    for hd in range(XA_HEADS):
        sl = slice(hd * XA_HEAD_DIM, (hd + 1) * XA_HEAD_DIM)
        kh = k_ref[0, :, sl].astype(BF16)
        vh = v_ref[0, :, sl].astype(BF16)
        s = _dot_nt(q[:, sl], kh)
        e = jnp.exp(s - jnp.max(s, axis=-1, keepdims=True))
        o = _dot(e.astype(BF16), vh) / jnp.sum(e, axis=-1, keepdims=True)
        out = out + _dot(o.astype(BF16), wo_ref[sl, :])
    y = x + out
    if final:
        y = _rmsnorm(y, gf_ref[...])
    xo_ref[0] = y


def _xattn_prompt(x, consts, k, v, layer, gf, tile, final):
    B, L, D = x.shape
    kern = functools.partial(_xattn_prompt_kernel, final=final)
    arrays, specs = _operands(consts)
    return pl.pallas_call(
        kern,
        grid=(B, L // tile),
        in_specs=[pl.BlockSpec((1, tile, D), lambda b, t: (b, t, 0))] + specs + [
            pl.BlockSpec((None, 1, N_MEM, D), lambda b, t: (layer, b, 0, 0)),
            pl.BlockSpec((None, 1, N_MEM, D), lambda b, t: (layer, b, 0, 0)),
            _const_spec(gf.shape)],
        out_specs=pl.BlockSpec((1, tile, D), lambda b, t: (b, t, 0)),
        out_shape=jax.ShapeDtypeStruct((B, L, D), F32),
        compiler_params=_params(("arbitrary", "arbitrary")),
        name="xattn_prompt",
    )(x, *arrays, k, v, gf)


def _odd_prompt_kernel(x_ref, g_ref, wz_ref, wxbc_ref, wdt_ref, cw_ref, cb_ref, dtb_ref, alog_ref,
                       dskip_ref, nw_ref, w_out_ref,
                       xo_ref, sconv_ref, state_ref, xext_ref, st_ref, y_ref, *, tile):
    t = pl.program_id(1)
    Q = SSM_CHUNK
    C = SSM_CONV_DIM

    @pl.when(t == 0)
    def _():
        xext_ref[0:HALO, :] = jnp.zeros((HALO, C), F32)
        st_ref[...] = jnp.zeros(st_ref.shape, F32)

    @pl.when(t > 0)
    def _():
        xext_ref[0:HALO, :] = xext_ref[tile:tile + HALO, :]

    x = x_ref[0]
    h = _rmsnorm(x, g_ref[...]).astype(BF16)
    xbc = _dot(h, wxbc_ref[...])
    xext_ref[HALO:HALO + tile, :] = xbc
    xe = xext_ref[...]
    conv = cw_ref[3:4, :] * xbc
    for k in range(1, SSM_CONV_K):
        conv = conv + cw_ref[3 - k:4 - k, :] * pltpu.roll(xe, k, 0)[HALO:, :]
    xc = _silu(conv + cb_ref[...])
    dt = _softplus(_dot(h, wdt_ref[...]) + dtb_ref[...])
    dta = dt * (-jnp.exp(alog_ref[...]))

    row = lax.broadcasted_iota(jnp.int32, (Q, Q), 0)
    col = lax.broadcasted_iota(jnp.int32, (Q, Q), 1)
    causal = row >= col
    tri = jnp.where(causal, 1.0, 0.0).astype(BF16)
    lo_lane = lax.broadcasted_iota(jnp.int32, (1, LANES), 1) < SSM_HEAD_DIM
    neg_inf = jnp.float32(-jnp.inf)

    for c in range(tile // Q):
        rs = slice(c * Q, (c + 1) * Q)
        a_cs = _dot_exact_rhs(tri, dta[rs, :])
        a_cs_t = a_cs.T
        dt_t = dt[rs, :].T
        w_t = dt_t * jnp.exp(a_cs_t[:, Q - 1:Q] - a_cs_t)
        ea = jnp.exp(a_cs)
        dec = ea[Q - 1:Q, :]
        for g in range(SSM_GROUPS):
            bsl = slice(D_INNER + g * SSM_STATE, D_INNER + (g + 1) * SSM_STATE)
            csl = slice(D_INNER + SSM_GROUPS * SSM_STATE + g * SSM_STATE,
                        D_INNER + SSM_GROUPS * SSM_STATE + (g + 1) * SSM_STATE)
            b_t = xc[rs, bsl].T
            c_g = xc[rs, csl].astype(BF16)
            cb = _dot(c_g, b_t.astype(BF16))
            gsl = slice(g * SSM_GROUP_W, (g + 1) * SSM_GROUP_W)
            y_off = _dot(c_g, st_ref[:, gsl].astype(BF16))
            for j in range(SSM_GROUP_W // LANES):
                h0 = (g * SSM_GROUP_W + j * LANES) // SSM_HEAD_DIM
                lsl = slice(g * SSM_GROUP_W + j * LANES, g * SSM_GROUP_W + (j + 1) * LANES)
                xs = xc[rs, lsl]
                wmat = jnp.concatenate([jnp.where(lo_lane, xs, 0.0), jnp.where(lo_lane, 0.0, xs)],
                                       axis=0).astype(BF16)
                tops, bots = [], []
                for hh in (h0, h0 + 1):
                    seg = a_cs[:, hh:hh + 1] - a_cs_t[hh:hh + 1, :]
                    decay = jnp.exp(jnp.where(causal, seg, neg_inf))
                    tops.append(cb * decay * dt_t[hh:hh + 1, :])
                    bots.append(b_t * w_t[hh:hh + 1, :])
                lhs = jnp.concatenate([jnp.concatenate(tops, axis=1), jnp.concatenate(bots, axis=1)],
                                      axis=0).astype(BF16)
                r = _dot(lhs, wmat)
                escale = jnp.where(lo_lane, ea[:, h0:h0 + 1], ea[:, h0 + 1:h0 + 2])
                y_ref[rs, lsl] = (r[0:Q, :] + y_off[:, j * LANES:(j + 1) * LANES] * escale
                                  + dskip_ref[:, lsl] * xs)
                dscale = jnp.where(lo_lane, dec[:, h0:h0 + 1], dec[:, h0 + 1:h0 + 2])
                st_ref[:, lsl] = st_ref[:, lsl] * dscale + r[Q:2 * Q, :]

    yg = y_ref[...] * _silu(_dot(h, wz_ref[...]))
    out = jnp.zeros((tile, D_MODEL), F32)
    for g in range(SSM_GROUPS):
        gsl = slice(g * SSM_GROUP_W, (g + 1) * SSM_GROUP_W)
        ygg = yg[:, gsl]
        yn = ygg * lax.rsqrt(jnp.mean(ygg * ygg, axis=-1, keepdims=True) + EPS) * nw_ref[:, gsl]
        out = out + _dot(yn.astype(BF16), w_out_ref[gsl, :])
    xo_ref[0] = x + out

    @pl.when(t == pl.num_programs(1) - 1)
    def _():
        sconv_ref[0] = xext_ref[tile:tile + HALO, :]
        state_ref[0] = st_ref[...].T


def _odd_prompt(x, consts, tile):
    B, L, D = x.shape
    kern = functools.partial(_odd_prompt_kernel, tile=tile)
    arrays, specs = _operands(consts)
    return pl.pallas_call(
        kern,
        grid=(B, L // tile),
        in_specs=[pl.BlockSpec((1, tile, D), lambda b, t: (b, t, 0))] + specs,
        out_specs=[pl.BlockSpec((1, tile, D), lambda b, t: (b, t, 0)),
                   pl.BlockSpec((1, HALO, SSM_CONV_DIM), lambda b, t: (b, 0, 0)),
                   pl.BlockSpec((1, D_INNER, SSM_STATE), lambda b, t: (b, 0, 0))],
        out_shape=[jax.ShapeDtypeStruct((B, L, D), F32),
                   jax.ShapeDtypeStruct((B, HALO, SSM_CONV_DIM), F32),
                   jax.ShapeDtypeStruct((B, D_INNER, SSM_STATE), F32)],
        scratch_shapes=[pltpu.VMEM((HALO + tile, SSM_CONV_DIM), F32),
                        pltpu.VMEM((SSM_STATE, D_INNER), F32),
                        pltpu.VMEM((tile, D_INNER), F32)],
        compiler_params=_params(("arbitrary", "arbitrary")),
        name="odd_prompt",
    )(x, *arrays)


def _even_sample_kernel(x_ref, g_ref, w_in_ref, pw_ref, ps_ref, cw_ref, w_out_ref, pool_ref, cst_ref,
                        xo_ref, pool_o_ref, cst_o_ref):
    D = D_MODEL
    x = x_ref[...]
    h = _rmsnorm(x, g_ref[...]).astype(BF16)

    def proj(j):
        return _dot(h, w_in_ref[:, j * D:(j + 1) * D])

    u = proj(0)
    g_pool = proj(1)
    out = jnp.zeros_like(x)
    for g, w in enumerate(POOL_WINDOWS):
        sl = slice(g * POOL_GROUP, (g + 1) * POOL_GROUP)
        s = u[:, sl]
        for j in range(POOL_BUF - (w - 1), POOL_BUF):
            s = s + pool_ref[j, :, sl]
        cnt = float(min(PAST_LEN + 1, w))
        diff = s / cnt - u[:, sl]
        yp = _dot(diff.astype(BF16), pw_ref[g]) * ps_ref[:, sl] * _silu(g_pool[:, sl])
        out = out + _dot(yp.astype(BF16), w_out_ref[sl, :])
    for j in range(POOL_BUF - 1):
        pool_o_ref[j] = pool_ref[j + 1]
    pool_o_ref[POOL_BUF - 1] = u

    b_gate = proj(2)
    cv = proj(3) * proj(4)
    conv = cw_ref[0:1, :] * cst_ref[0] + cw_ref[1:2, :] * cst_ref[1] + cw_ref[2:3, :] * cv
    cst_o_ref[0] = cst_ref[1]
    cst_o_ref[1] = cv
    yc = b_gate * conv * _silu(proj(5))
    out = out + _dot(yc.astype(BF16), w_out_ref[D:2 * D, :])
    xo_ref[...] = x + out


def _even_sample(x, consts, pool_t, cst_t):
    args, specs = _operands([x] + list(consts) + [pool_t, cst_t])
    outs = (x, pool_t, cst_t)
    return pl.pallas_call(
        _even_sample_kernel,
        grid=(1,),
        in_specs=specs,
        out_specs=[_const_spec(a.shape) for a in outs],
        out_shape=[jax.ShapeDtypeStruct(a.shape, F32) for a in outs],
        compiler_params=_params(("arbitrary",)),
        name="even_sample",
    )(*args)


def _xattn_sample_q_kernel(x_ref, g_ref, wq_ref, q_ref):
    h = _rmsnorm(x_ref[...], g_ref[...]).astype(BF16)
    q_ref[...] = _dot(h, wq_ref[...]) * (XA_HEAD_DIM ** -0.5)


def _xattn_sample_q(x, g, wq):
    args, specs = _operands((x, g, wq))
    return pl.pallas_call(
        _xattn_sample_q_kernel,
        grid=(1,),
        in_specs=specs,
        out_specs=_const_spec(x.shape),
        out_shape=jax.ShapeDtypeStruct(x.shape, F32),
        compiler_params=_params(("arbitrary",)),
        name="xattn_sample_q",
    )(*args)


def _xattn_sample_stream_kernel(q_ref, k_ref, v_ref, o_ref, *, bt):
    for b in range(bt):
        r = jnp.sum(k_ref[b] * q_ref[b][None], axis=-1, keepdims=True)
        s = r + pltpu.roll(r, XA_HEADS, 1)
        e = jnp.exp(s - jnp.max(s, axis=0, keepdims=True))
        p = e / jnp.sum(e, axis=0, keepdims=True)
        o_ref[b] = jnp.sum(p * v_ref[b], axis=0)


def _xattn_sample_stream(q8, k8, v8, layer, bt):
    DB = q8.shape[0]
    slab = q8.shape[1:]
    kern = functools.partial(_xattn_sample_stream_kernel, bt=bt)
    return pl.pallas_call(
        kern,
        grid=(DB // bt,),
        in_specs=[pl.BlockSpec((bt,) + slab, lambda i: (i, 0, 0)),
                  pl.BlockSpec((None, bt, N_MEM) + slab, lambda i: (layer, i, 0, 0, 0)),
                  pl.BlockSpec((None, bt, N_MEM) + slab, lambda i: (layer, i, 0, 0, 0))],
        out_specs=pl.BlockSpec((bt,) + slab, lambda i: (i, 0, 0)),
        out_shape=jax.ShapeDtypeStruct(q8.shape, F32),
        compiler_params=_params(("arbitrary",)),
        name="xattn_sample_stream",
    )(q8, k8, v8)


def _xattn_sample_out_kernel(x_ref, o_ref, wo_ref, gf_ref, xo_ref, *, final):
    y = x_ref[...] + _dot(o_ref[...].astype(BF16), wo_ref[...])
    if final:
        y = _rmsnorm(y, gf_ref[...])
    xo_ref[...] = y


def _xattn_sample_out(x, o, wo, gf, final):
    args, specs = _operands((x, o, wo, gf))
    kern = functools.partial(_xattn_sample_out_kernel, final=final)
    return pl.pallas_call(
        kern,
        grid=(1,),
        in_specs=specs,
        out_specs=_const_spec(x.shape),
        out_shape=jax.ShapeDtypeStruct(x.shape, F32),
        compiler_params=_params(("arbitrary",)),
        name="xattn_sample_out",
    )(*args)


def _odd_sample_in_kernel(x_ref, g_ref, wz_ref, wxbc_ref, wdt_ref, cw_ref, cb_ref, dtb_ref, alog_ref, sconv_ref,
                          z_ref, xs_ref, bm_ref, cm_ref, xdt_ref, dec_ref, sconv_o_ref):
    h = _rmsnorm(x_ref[...], g_ref[...]).astype(BF16)
    xbc = _dot(h, wxbc_ref[...])
    conv = cw_ref[3:4, :] * xbc
    for k in range(SSM_CONV_K - 1):
        conv = conv + cw_ref[k:k + 1, :] * sconv_ref[k]
    for k in range(SSM_CONV_K - 2):
        sconv_o_ref[k] = sconv_ref[k + 1]
    sconv_o_ref[SSM_CONV_K - 2] = xbc
    xc = _silu(conv + cb_ref[...])
    xs = xc[:, :D_INNER]
    dt = _softplus(_dot(h, wdt_ref[...]) + dtb_ref[...])
    dta = dt * (-jnp.exp(alog_ref[...]))
    hrow = lax.broadcasted_iota(jnp.int32, (LANES, D_INNER), 0)
    ccol = lax.broadcasted_iota(jnp.int32, (LANES, D_INNER), 1)
    expand = jnp.where(ccol // SSM_HEAD_DIM == hrow, 1.0, 0.0).astype(BF16)
    z_ref[...] = _dot(h, wz_ref[...])
    xs_ref[...] = xs
    bm_ref[...] = xc[:, D_INNER:D_INNER + SSM_GROUPS * SSM_STATE]
    cm_ref[...] = xc[:, D_INNER + SSM_GROUPS * SSM_STATE:]
    xdt_ref[...] = xs * _dot_exact_lhs(dt, expand)
    dec_ref[...] = jnp.exp(dta)


def _odd_sample_in(x, consts, sconv_t):
    DB = x.shape[0]
    args, specs = _operands([x] + list(consts) + [sconv_t])
    shapes = [(DB, D_INNER), (DB, D_INNER), (DB, SSM_GROUPS * SSM_STATE), (DB, SSM_GROUPS * SSM_STATE),
              (DB, D_INNER), (DB, LANES), sconv_t.shape]
    return pl.pallas_call(
        _odd_sample_in_kernel,
        grid=(1,),
        in_specs=specs,
        out_specs=[_const_spec(s) for s in shapes],
        out_shape=[jax.ShapeDtypeStruct(s, F32) for s in shapes],
        compiler_params=_params(("arbitrary",)),
        name="odd_sample_in",
    )(*args)


def _odd_sample_state_kernel(dec_ref, xdt_ref, bm_ref, cm_ref, s_ref, *rest, bt):
    so_ref, y_ref = rest[-2:]
    i = pl.program_id(0)
    pad = jnp.zeros((LANES - bt, D_INNER), F32)
    xdt_t = jnp.concatenate([xdt_ref[...], pad], axis=0).T.astype(BF16)
    padn = jnp.zeros((LANES - bt, SSM_STATE), F32)
    row = lax.broadcasted_iota(jnp.int32, (LANES, SSM_STATE), 0)
    y_acc = [jnp.zeros((SSM_GROUP_W, LANES), F32) for _ in range(SSM_GROUPS)]
    heads_per_group = SSM_GROUP_W // SSM_HEAD_DIM
    for g in range(SSM_GROUPS):
        nsl = slice(g * SSM_STATE, (g + 1) * SSM_STATE)
        gsl = slice(g * SSM_GROUP_W, (g + 1) * SSM_GROUP_W)
        b_all = jnp.concatenate([bm_ref[:, nsl], padn], axis=0)
        c_all = jnp.concatenate([cm_ref[:, nsl], padn], axis=0)
        for b in range(bt):
            b_sel = jnp.where(row == b, b_all, 0.0).astype(BF16)
            c_sel = jnp.where(row == b, c_all, 0.0).astype(BF16)
            upd = _dot(xdt_t[gsl, :], b_sel)
            pieces = []
            for e in range(heads_per_group):
                hd = g * heads_per_group + e
                rsl = slice(hd * SSM_HEAD_DIM, (hd + 1) * SSM_HEAD_DIM)
                new = s_ref[b, rsl, :] * dec_ref[i * bt + b, hd] + upd[e * SSM_HEAD_DIM:(e + 1) * SSM_HEAD_DIM, :]
                so_ref[b, rsl, :] = new
                pieces.append(new)
            y_acc[g] = y_acc[g] + _dot_nt(jnp.concatenate(pieces, axis=0).astype(BF16), c_sel)
    y_ref[...] = jnp.concatenate(y_acc, axis=0).T[0:bt, :]


def _odd_sample_state(dec, xdt, bm, cm, state_all, layer, prev_out, bt):
    DB = xdt.shape[0]
    kern = functools.partial(_odd_sample_state_kernel, bt=bt)
    gn = SSM_GROUPS * SSM_STATE
    in_specs = [pl.BlockSpec(memory_space=pltpu.SMEM),
                pl.BlockSpec((bt, D_INNER), lambda i: (i, 0)),
                pl.BlockSpec((bt, gn), lambda i: (i, 0)),
                pl.BlockSpec((bt, gn), lambda i: (i, 0)),
                pl.BlockSpec((None, bt, D_INNER, SSM_STATE), lambda i: (layer, i, 0, 0))]
    args = [dec, xdt, bm, cm, state_all]
    aliases = {}
    if prev_out is not None:
        in_specs.append(pl.BlockSpec(memory_space=pl.ANY))
        args.append(prev_out)
        aliases = {len(args) - 1: 0}
    return pl.pallas_call(
        kern,
        grid=(DB // bt,),
        in_specs=in_specs,
        out_specs=[pl.BlockSpec((None, bt, D_INNER, SSM_STATE), lambda i: (layer, i, 0, 0)),
                   pl.BlockSpec((bt, D_INNER), lambda i: (i, 0))],
        out_shape=[jax.ShapeDtypeStruct(state_all.shape, F32), jax.ShapeDtypeStruct((DB, D_INNER), F32)],
        input_output_aliases=aliases,
        compiler_params=_params(("arbitrary",)),
        name="odd_sample_state",
    )(*args)


def _odd_sample_out_kernel(x_ref, y_ref, xs_ref, z_ref, dskip_ref, nw_ref, w_out_ref, xo_ref):
    yg = (y_ref[...] + dskip_ref[...] * xs_ref[...]) * _silu(z_ref[...])
    out = jnp.zeros(x_ref.shape, F32)
    for g in range(SSM_GROUPS):
        gsl = slice(g * SSM_GROUP_W, (g + 1) * SSM_GROUP_W)
        ygg = yg[:, gsl]
        yn = ygg * lax.rsqrt(jnp.mean(ygg * ygg, axis=-1, keepdims=True) + EPS) * nw_ref[:, gsl]
        out = out + _dot(yn.astype(BF16), w_out_ref[gsl, :])
    xo_ref[...] = x_ref[...] + out


def _odd_sample_out(x, y, xs, z, dskip, nw, w_out):
    args, specs = _operands((x, y, xs, z, dskip, nw, w_out))
    return pl.pallas_call(
        _odd_sample_out_kernel,
        grid=(1,),
        in_specs=specs,
        out_specs=_const_spec(x.shape),
        out_shape=jax.ShapeDtypeStruct(x.shape, F32),
        compiler_params=_params(("arbitrary",)),
        name="odd_sample_out",
    )(*args)


def _pad_lanes(v):
    return jnp.pad(v, ((0, 0), (0, LANES - v.shape[-1])))


def kernel(x_prompt, x_sample, mem_prompt, cache_mem_k, cache_mem_v, state_pool, state_conv, state_ssm_conv,
           state_ssm, norm_mix, norm_xa, norm_final, w_in_even, pool_w, pool_scale, conv_w, w_out_even, w_in_odd,
           ssm_conv_w, ssm_conv_b, dt_bias, a_log, d_skip, ssm_norm, w_out_odd, w_xa_q, w_xa_k, w_xa_v, w_xa_o):
    B, L, D = x_prompt.shape
    DB = x_sample.shape[0]
    depth = norm_mix.shape[0]
    tile = min(L, 512)
    tile_odd = min(L, 256)
    bt_x = min(DB, 8)
    bt_s = min(DB, 8)

    mem_k, mem_v, mem_k8, mem_v8 = _kv_proj(mem_prompt.reshape(B * N_MEM, D), w_xa_k, w_xa_v)
    mem_k = mem_k.reshape(depth, B, N_MEM, D)
    mem_v = mem_v.reshape(depth, B, N_MEM, D)
    slab = (D // LANES, LANES)
    cache_k8 = _to_head_slabs(cache_mem_k)
    cache_v8 = _to_head_slabs(cache_mem_v)
    ssm_all = state_ssm.reshape(state_ssm.shape[0], DB, D_INNER, SSM_STATE)
    ssm_s = None

    def rows(a):
        return a.reshape(a.shape[0], 1, a.shape[-1])

    g_mix_all, g_xa_all = rows(norm_mix), rows(norm_xa)
    w_in_e, pw_e, ps_e = w_in_even.astype(BF16), pool_w.astype(BF16), rows(pool_scale)
    w_out_e = w_out_even.astype(BF16)
    wz_o = w_in_odd[:, :, :D_INNER].astype(BF16)
    wxbc_o = w_in_odd[:, :, D_INNER:D_INNER + SSM_CONV_DIM].astype(BF16)
    wdt_o = jnp.pad(w_in_odd[:, :, D_INNER + SSM_CONV_DIM:], ((0, 0), (0, 0), (0, LANES - SSM_HEADS))).astype(BF16)
    cb_o, dtb_o, alog_o = rows(ssm_conv_b), rows(_pad_lanes(dt_bias)), rows(_pad_lanes(a_log))
    dskip_o, nw_o = rows(jnp.repeat(d_skip, SSM_HEAD_DIM, axis=1)), rows(ssm_norm)
    w_out_o = w_out_odd.astype(BF16)
    wq_all, wo_all = w_xa_q.astype(BF16), w_xa_o.astype(BF16)

    xp = x_prompt
    xs_ = x_sample.reshape(DB, D)
    gf = norm_final.reshape(1, D)
    pool_p, pool_s, conv_p, conv_s, sconv_p, sconv_s, ssm_p = [], [], [], [], [], [], []
    for l in range(depth):
        i = l // 2
        g_mix = _layer(g_mix_all, l)
        if l % 2 == 0:
            consts = [g_mix, _layer(w_in_e, i), _layer(pw_e, i), _layer(ps_e, i), _layer(conv_w, i),
                      _layer(w_out_e, i)]
            xp, pb, cbuf = _even_prompt(xp, consts, tile)
            pool_p.append(pb[:, HALO - POOL_BUF:, :])
            conv_p.append(cbuf[:, HALO - (CONV_K - 1):, :])
            xs_, pool_t, cst_t = _even_sample(xs_, consts, state_pool[i].transpose(1, 0, 2),
                                              state_conv[i].transpose(1, 0, 2))
            pool_s.append(pool_t.transpose(1, 0, 2))
            conv_s.append(cst_t.transpose(1, 0, 2))
        else:
            in_consts = [g_mix, _layer(wz_o, i), _layer(wxbc_o, i), _layer(wdt_o, i), _layer(ssm_conv_w, i),
                         _layer(cb_o, i), _layer(dtb_o, i), _layer(alog_o, i)]
            dskip, nw, w_out = _layer(dskip_o, i), _layer(nw_o, i), _layer(w_out_o, i)
            xp, sb, st = _odd_prompt(xp, in_consts + [dskip, nw, w_out], tile_odd)
            sconv_p.append(sb[:, HALO - (SSM_CONV_K - 1):, :])
            ssm_p.append(st.reshape(B, SSM_HEADS, SSM_HEAD_DIM, SSM_STATE))
            z, xs_in, bm, cm, xdt, dec, sconv_t = _odd_sample_in(
                xs_, in_consts, state_ssm_conv[i].transpose(1, 0, 2))
            ssm_s, y = _odd_sample_state(dec, xdt, bm, cm, ssm_all, i, ssm_s, bt_s)
            xs_ = _odd_sample_out(xs_, y, xs_in, z, dskip, nw, w_out)
            sconv_s.append(sconv_t.transpose(1, 0, 2))
        g_xa, wq, wo = _layer(g_xa_all, l), _layer(wq_all, l), _layer(wo_all, l)
        final = l == depth - 1
        xp = _xattn_prompt(xp, [g_xa, wq, wo], mem_k, mem_v, l, gf, tile, final)
        q8 = _to_head_slabs(_xattn_sample_q(xs_, g_xa, wq).reshape(DB, XA_HEADS, XA_HEAD_DIM))
        o8 = _xattn_sample_stream(q8, cache_k8, cache_v8, l, bt_x)
        xs_ = _xattn_sample_out(xs_, _from_head_slabs(o8).reshape(DB, D), wo, gf, final)

    mem_shape = (depth, B, N_MEM) + slab
    return (xp, xs_.reshape(DB, 1, D),
            _from_head_slabs(mem_k8.reshape(mem_shape)), _from_head_slabs(mem_v8.reshape(mem_shape)),
            jnp.stack(pool_p), jnp.stack(pool_s), jnp.stack(conv_p), jnp.stack(conv_s),
            jnp.stack(sconv_p), jnp.stack(sconv_s), jnp.stack(ssm_p),
            ssm_s.reshape(ssm_all.shape[0], DB, SSM_HEADS, SSM_HEAD_DIM, SSM_STATE))
```

```python
import functools

import jax
import jax.numpy as jnp
from jax import lax
from jax.experimental import pallas as pl
from jax.experimental.pallas import tpu as pltpu

F32 = jnp.float32
BF16 = jnp.bfloat16

D_MODEL = 1024
PAST_LEN = 16384
POOL_WINDOWS = (2, 4, 8, 16)
POOL_GROUP = D_MODEL // len(POOL_WINDOWS)
POOL_BUF = max(POOL_WINDOWS) - 1
CONV_K = 3
D_INNER = 2 * D_MODEL
SSM_HEAD_DIM = 64
SSM_HEADS = D_INNER // SSM_HEAD_DIM
SSM_GROUPS = 4
SSM_STATE = 128
SSM_GROUP_W = D_INNER // SSM_GROUPS
SSM_CONV_K = 4
SSM_CONV_DIM = D_INNER + 2 * SSM_GROUPS * SSM_STATE
SSM_CHUNK = 128
N_MEM = 256
XA_HEADS = 4
XA_HEAD_DIM = D_MODEL // XA_HEADS
EPS = 1e-6
LOG2E = 1.4426950408889634
LANES = 128
HALO = 16
VMEM_LIMIT = 56 * 1024 * 1024


def _rmsnorm(x, g):
    return x * lax.rsqrt(jnp.mean(x * x, axis=-1, keepdims=True) + EPS) * g


def _silu(x):
    return x * jax.nn.sigmoid(x)


def _softplus(x):
    return jnp.maximum(x, 0.0) + jnp.log1p(jnp.exp(-jnp.abs(x)))


def _dot(a, b):
    return jnp.dot(a, b, preferred_element_type=F32)


def _dot_nt(a, b):
    return lax.dot_general(a, b, (((1,), (1,)), ((), ())), preferred_element_type=F32)


def _split3(x):
    hi = x.astype(BF16)
    r1 = x - hi.astype(F32)
    mid = r1.astype(BF16)
    lo = (r1 - mid.astype(F32)).astype(BF16)
    return hi, mid, lo


def _dot_exact_rhs(a_bf16, x):
    hi, mid, lo = _split3(x)
    return _dot(a_bf16, hi) + _dot(a_bf16, mid) + _dot(a_bf16, lo)


def _dot_exact_lhs(x, b_bf16):
    hi, mid, lo = _split3(x)
    return _dot(hi, b_bf16) + _dot(mid, b_bf16) + _dot(lo, b_bf16)


def _const_spec(shape):
    nd = len(shape)
    return pl.BlockSpec(shape, lambda *_: (0,) * nd, pipeline_mode=pl.Buffered(1))


def _layer(arr, l):
    return (arr, l)


def _operands(ops):
    arrays, specs = [], []
    for op in ops:
        if isinstance(op, tuple):
            arr, l = op
            nd = arr.ndim - 1
            specs.append(pl.BlockSpec((None,) + arr.shape[1:], lambda *_, l=l, nd=nd: (l,) + (0,) * nd,
                                      pipeline_mode=pl.Buffered(1)))
        else:
            arr = op
            specs.append(_const_spec(arr.shape))
        arrays.append(arr)
    return arrays, specs


def _params(sem):
    return pltpu.CompilerParams(dimension_semantics=sem, vmem_limit_bytes=VMEM_LIMIT)


def _head_rows_order():
    lane_tiles = XA_HEAD_DIM // LANES
    return [(r % XA_HEADS) * lane_tiles + r // XA_HEADS for r in range(XA_HEADS * lane_tiles)]


def _to_head_slabs(a):
    lead = a.shape[:-2]
    lane_tiles = XA_HEAD_DIM // LANES
    nd = len(lead)
    a = a.reshape(lead + (XA_HEADS, lane_tiles, LANES))
    a = a.transpose(tuple(range(nd)) + (nd + 1, nd, nd + 2))
    return a.reshape(lead + (XA_HEADS * lane_tiles, LANES))


def _from_head_slabs(a):
    lead = a.shape[:-2]
    lane_tiles = XA_HEAD_DIM // LANES
    nd = len(lead)
    a = a.reshape(lead + (lane_tiles, XA_HEADS, LANES))
    a = a.transpose(tuple(range(nd)) + (nd + 1, nd, nd + 2))
    return a.reshape(lead + (XA_HEADS, XA_HEAD_DIM))


def _kv_proj_kernel(mem_ref, wk_ref, wv_ref, k_ref, v_ref, k8_ref, v8_ref):
    m = mem_ref[...].astype(BF16)
    k = _dot(m, wk_ref[0].astype(BF16))
    v = _dot(m, wv_ref[0].astype(BF16))
    k_ref[0] = k
    v_ref[0] = v
    for r, j in enumerate(_head_rows_order()):
        k8_ref[0, :, r, :] = k[:, j * LANES:(j + 1) * LANES]
        v8_ref[0, :, r, :] = v[:, j * LANES:(j + 1) * LANES]


def _kv_proj(mem2d, w_k, w_v):
    rows = mem2d.shape[0]
    depth = w_k.shape[0]
    tr = min(rows, 512)
    out = jax.ShapeDtypeStruct((depth, rows, D_MODEL), F32)
    out8 = jax.ShapeDtypeStruct((depth, rows, D_MODEL // LANES, LANES), F32)
    return pl.pallas_call(
        _kv_proj_kernel,
        grid=(depth, rows // tr),
        in_specs=[pl.BlockSpec((tr, D_MODEL), lambda l, r: (r, 0)),
                  pl.BlockSpec((1, D_MODEL, D_MODEL), lambda l, r: (l, 0, 0)),
                  pl.BlockSpec((1, D_MODEL, D_MODEL), lambda l, r: (l, 0, 0))],
        out_specs=[pl.BlockSpec((1, tr, D_MODEL), lambda l, r: (l, r, 0)),
                   pl.BlockSpec((1, tr, D_MODEL), lambda l, r: (l, r, 0)),
                   pl.BlockSpec((1, tr, D_MODEL // LANES, LANES), lambda l, r: (l, r, 0, 0)),
                   pl.BlockSpec((1, tr, D_MODEL // LANES, LANES), lambda l, r: (l, r, 0, 0))],
        out_shape=[out, out, out8, out8],
        compiler_params=_params(("arbitrary", "arbitrary")),
        name="kv_proj",
    )(mem2d, w_k, w_v)


def _even_prompt_kernel(x_ref, g_ref, w_in_ref, pw_ref, ps_ref, cw_ref, w_out_ref,
                        xo_ref, pool_ref, conv_ref, ext_ref, cext_ref, *, tile):
    t = pl.program_id(1)
    D = D_MODEL

    @pl.when(t == 0)
    def _():
        ext_ref[0:HALO, :] = jnp.zeros((HALO, D), F32)
        cext_ref[0:HALO, :] = jnp.zeros((HALO, D), F32)

    @pl.when(t > 0)
    def _():
        ext_ref[0:HALO, :] = ext_ref[tile:tile + HALO, :]
        cext_ref[0:HALO, :] = cext_ref[tile:tile + HALO, :]

    x = x_ref[0]
    h = _rmsnorm(x, g_ref[...]).astype(BF16)

    def proj(j):
        return _dot(h, w_in_ref[:, j * D:(j + 1) * D])

    u = proj(0)
    ext_ref[HALO:HALO + tile, :] = u
    g_pool = proj(1)
    pos = t * tile + lax.broadcasted_iota(jnp.int32, (tile, POOL_GROUP), 0)
    out = jnp.zeros((tile, D), F32)
    for g, w in enumerate(POOL_WINDOWS):
        sl = slice(g * POOL_GROUP, (g + 1) * POOL_GROUP)
        s = ext_ref[:, sl]
        k = 1
        while k < w:
            s = s + pltpu.roll(s, k, 0)
            k *= 2
        cnt = jnp.minimum(pos + 1, w).astype(F32)
        diff = s[HALO:, :] / cnt - u[:, sl]
        yp = _dot(diff.astype(BF16), pw_ref[g]) * ps_ref[:, sl] * _silu(g_pool[:, sl])
        out = out + _dot(yp.astype(BF16), w_out_ref[sl, :])

    b_gate = proj(2)
    cv = proj(3) * proj(4)
    cext_ref[HALO:HALO + tile, :] = cv
    ce = cext_ref[...]
    conv = cw_ref[2:3, :] * cv
    conv = conv + cw_ref[1:2, :] * pltpu.roll(ce, 1, 0)[HALO:, :]
    conv = conv + cw_ref[0:1, :] * pltpu.roll(ce, 2, 0)[HALO:, :]
    yc = b_gate * conv * _silu(proj(5))
    out = out + _dot(yc.astype(BF16), w_out_ref[D:2 * D, :])
    xo_ref[0] = x + out

    @pl.when(t == pl.num_programs(1) - 1)
    def _():
        pool_ref[0] = ext_ref[tile:tile + HALO, :]
        conv_ref[0] = cext_ref[tile:tile + HALO, :]


def _even_prompt(x, consts, tile):
    B, L, D = x.shape
    kern = functools.partial(_even_prompt_kernel, tile=tile)
    arrays, specs = _operands(consts)
    return pl.pallas_call(
        kern,
        grid=(B, L // tile),
        in_specs=[pl.BlockSpec((1, tile, D), lambda b, t: (b, t, 0))] + specs,
        out_specs=[pl.BlockSpec((1, tile, D), lambda b, t: (b, t, 0)),
                   pl.BlockSpec((1, HALO, D), lambda b, t: (b, 0, 0)),
                   pl.BlockSpec((1, HALO, D), lambda b, t: (b, 0, 0))],
        out_shape=[jax.ShapeDtypeStruct((B, L, D), F32),
                   jax.ShapeDtypeStruct((B, HALO, D), F32),
                   jax.ShapeDtypeStruct((B, HALO, D), F32)],
        scratch_shapes=[pltpu.VMEM((HALO + tile, D), F32), pltpu.VMEM((HALO + tile, D), F32)],
        compiler_params=_params(("arbitrary", "arbitrary")),
        name="even_prompt",
    )(x, *arrays)


def _xattn_prompt_kernel(x_ref, g_ref, wq_ref, wo_ref, k_ref, v_ref, gf_ref, xo_ref, *, final):
    x = x_ref[0]
    h = _rmsnorm(x, g_ref[...]).astype(BF16)
    q = (_dot(h, wq_ref[...]) * (XA_HEAD_DIM ** -0.5)).astype(BF16)
    out = jnp.zeros_like(x)
    for hd in range(XA_HEADS):
        sl = slice(hd * XA_HEAD_DIM, (hd + 1) * XA_HEAD_DIM)
        kh = k_ref[0, :, sl].astype(BF16)
        vh = v_ref[0, :, sl].astype(BF16)
        s = _dot_nt(q[:, sl], kh)
        e = jnp.exp(s - jnp.max(s, axis=-1, keepdims=True))
        o = _dot(e.astype(BF16), vh) / jnp.sum(e, axis=-1, keepdims=True)
        out = out + _dot(o.astype(BF16), wo_ref[sl, :])
    y = x + out
    if final:
        y = _rmsnorm(y, gf_ref[...])
    xo_ref[0] = y


def _xattn_prompt(x, consts, k, v, layer, gf, tile, final):
    B, L, D = x.shape
    kern = functools.partial(_xattn_prompt_kernel, final=final)
    arrays, specs = _operands(consts)
    return pl.pallas_call(
        kern,
        grid=(B, L // tile),
        in_specs=[pl.BlockSpec((1, tile, D), lambda b, t: (b, t, 0))] + specs + [
            pl.BlockSpec((None, 1, N_MEM, D), lambda b, t: (layer, b, 0, 0)),
            pl.BlockSpec((None, 1, N_MEM, D), lambda b, t: (layer, b, 0, 0)),
            _const_spec(gf.shape)],
        out_specs=pl.BlockSpec((1, tile, D), lambda b, t: (b, t, 0)),
        out_shape=jax.ShapeDtypeStruct((B, L, D), F32),
        compiler_params=_params(("arbitrary", "arbitrary")),
        name="xattn_prompt",
    )(x, *arrays, k, v, gf)


def _odd_prompt_kernel(x_ref, g_ref, wz_ref, wxbc_ref, wdt_ref, cw_ref, cb_ref, dtb_ref, alog_ref,
                       dskip_ref, nw_ref, w_out_ref,
                       xo_ref, sconv_ref, state_ref, xext_ref, st_ref, y_ref, *, tile):
    t = pl.program_id(1)
    Q = SSM_CHUNK
    C = SSM_CONV_DIM

    @pl.when(t == 0)
    def _():
        xext_ref[0:HALO, :] = jnp.zeros((HALO, C), F32)
        st_ref[...] = jnp.zeros(st_ref.shape, F32)

    @pl.when(t > 0)
    def _():
        xext_ref[0:HALO, :] = xext_ref[tile:tile + HALO, :]

    x = x_ref[0]
    h = _rmsnorm(x, g_ref[...]).astype(BF16)
    xbc = _dot(h, wxbc_ref[...])
    xext_ref[HALO:HALO + tile, :] = xbc
    xe = xext_ref[...]
    conv = cw_ref[3:4, :] * xbc
    for k in range(1, SSM_CONV_K):
        conv = conv + cw_ref[3 - k:4 - k, :] * pltpu.roll(xe, k, 0)[HALO:, :]
    xc = _silu(conv + cb_ref[...])
    dt = _softplus(_dot(h, wdt_ref[...]) + dtb_ref[...])
    dta = dt * (-jnp.exp(alog_ref[...]) * LOG2E)

    row = lax.broadcasted_iota(jnp.int32, (Q, Q), 0)
    col = lax.broadcasted_iota(jnp.int32, (Q, Q), 1)
    causal = row >= col
    tri = jnp.where(causal, 1.0, 0.0).astype(BF16)
    lo_lane = lax.broadcasted_iota(jnp.int32, (1, LANES), 1) < SSM_HEAD_DIM
    neg_inf = jnp.float32(-jnp.inf)

    for c in range(tile // Q):
        rs = slice(c * Q, (c + 1) * Q)
        a_cs = _dot_exact_rhs(tri, dta[rs, :])
        a_cs_t = a_cs.T
        dt_t = dt[rs, :].T
        w_t = dt_t * jnp.exp2(a_cs_t[:, Q - 1:Q] - a_cs_t)
        ea = jnp.exp2(a_cs)
        dec = ea[Q - 1:Q, :]
        for g in range(SSM_GROUPS):
            bsl = slice(D_INNER + g * SSM_STATE, D_INNER + (g + 1) * SSM_STATE)
            csl = slice(D_INNER + SSM_GROUPS * SSM_STATE + g * SSM_STATE,
                        D_INNER + SSM_GROUPS * SSM_STATE + (g + 1) * SSM_STATE)
            b_t = xc[rs, bsl].T
            c_g = xc[rs, csl].astype(BF16)
            cb = _dot(c_g, b_t.astype(BF16))
            gsl = slice(g * SSM_GROUP_W, (g + 1) * SSM_GROUP_W)
            y_off = _dot(c_g, st_ref[:, gsl].astype(BF16))
            for j in range(SSM_GROUP_W // LANES):
                h0 = (g * SSM_GROUP_W + j * LANES) // SSM_HEAD_DIM
                lsl = slice(g * SSM_GROUP_W + j * LANES, g * SSM_GROUP_W + (j + 1) * LANES)
                xs = xc[rs, lsl]
                wmat = jnp.concatenate([jnp.where(lo_lane, xs, 0.0), jnp.where(lo_lane, 0.0, xs)],
                                       axis=0).astype(BF16)
                tops, bots = [], []
                for hh in (h0, h0 + 1):
                    seg = a_cs[:, hh:hh + 1] - a_cs_t[hh:hh + 1, :]
                    decay = jnp.exp2(jnp.where(causal, seg, neg_inf))
                    tops.append(cb * decay * dt_t[hh:hh + 1, :])
                    bots.append(b_t * w_t[hh:hh + 1, :])
                lhs = jnp.concatenate([jnp.concatenate(tops, axis=1), jnp.concatenate(bots, axis=1)],
                                      axis=0).astype(BF16)
                r = _dot(lhs, wmat)
                escale = jnp.where(lo_lane, ea[:, h0:h0 + 1], ea[:, h0 + 1:h0 + 2])
                y_ref[rs, lsl] = (r[0:Q, :] + y_off[:, j * LANES:(j + 1) * LANES] * escale
                                  + dskip_ref[:, lsl] * xs)
                dscale = jnp.where(lo_lane, dec[:, h0:h0 + 1], dec[:, h0 + 1:h0 + 2])
                st_ref[:, lsl] = st_ref[:, lsl] * dscale + r[Q:2 * Q, :]

    yg = y_ref[...] * _silu(_dot(h, wz_ref[...]))
    out = jnp.zeros((tile, D_MODEL), F32)
    for g in range(SSM_GROUPS):
        gsl = slice(g * SSM_GROUP_W, (g + 1) * SSM_GROUP_W)
        ygg = yg[:, gsl]
        yn = ygg * lax.rsqrt(jnp.mean(ygg * ygg, axis=-1, keepdims=True) + EPS) * nw_ref[:, gsl]
        out = out + _dot(yn.astype(BF16), w_out_ref[gsl, :])
    xo_ref[0] = x + out

    @pl.when(t == pl.num_programs(1) - 1)
    def _():
        sconv_ref[0] = xext_ref[tile:tile + HALO, :]
        state_ref[0] = st_ref[...].T


def _odd_prompt(x, consts, tile):
    B, L, D = x.shape
    kern = functools.partial(_odd_prompt_kernel, tile=tile)
    arrays, specs = _operands(consts)
    return pl.pallas_call(
        kern,
        grid=(B, L // tile),
        in_specs=[pl.BlockSpec((1, tile, D), lambda b, t: (b, t, 0))] + specs,
        out_specs=[pl.BlockSpec((1, tile, D), lambda b, t: (b, t, 0)),
                   pl.BlockSpec((1, HALO, SSM_CONV_DIM), lambda b, t: (b, 0, 0)),
                   pl.BlockSpec((1, D_INNER, SSM_STATE), lambda b, t: (b, 0, 0))],
        out_shape=[jax.ShapeDtypeStruct((B, L, D), F32),
                   jax.ShapeDtypeStruct((B, HALO, SSM_CONV_DIM), F32),
                   jax.ShapeDtypeStruct((B, D_INNER, SSM_STATE), F32)],
        scratch_shapes=[pltpu.VMEM((HALO + tile, SSM_CONV_DIM), F32),
                        pltpu.VMEM((SSM_STATE, D_INNER), F32),
                        pltpu.VMEM((tile, D_INNER), F32)],
        compiler_params=_params(("arbitrary", "arbitrary")),
        name="odd_prompt",
    )(x, *arrays)


def _finish_xattn(x, o8_ref, wo_ref):
    for r, j in enumerate(_head_rows_order()):
        x = x + _dot(o8_ref[:, r, :].astype(BF16), wo_ref[j * LANES:(j + 1) * LANES, :])
    return x


def _emit_query(x, g_ref, wq_ref, q8_ref):
    q = _dot(_rmsnorm(x, g_ref[...]).astype(BF16), wq_ref[...]) * (XA_HEAD_DIM ** -0.5)
    for r, j in enumerate(_head_rows_order()):
        q8_ref[:, r, :] = q[:, j * LANES:(j + 1) * LANES]


def _even_sample_kernel(*refs, has_prev):
    refs = list(refs)
    x_ref = refs.pop(0)
    x = x_ref[...]
    if has_prev:
        o8_ref, wo_ref = refs.pop(0), refs.pop(0)
        x = _finish_xattn(x, o8_ref, wo_ref)
    (g_ref, w_in_ref, pw_ref, ps_ref, cw_ref, w_out_ref, pool_ref, cst_ref, gq_ref, wq_ref,
     xo_ref, q8_ref, pool_o_ref, cst_o_ref) = refs
    D = D_MODEL
    h = _rmsnorm(x, g_ref[...]).astype(BF16)

    def proj(j):
        return _dot(h, w_in_ref[:, j * D:(j + 1) * D])

    u = proj(0)
    g_pool = proj(1)
    out = jnp.zeros_like(x)
    for g, w in enumerate(POOL_WINDOWS):
        sl = slice(g * POOL_GROUP, (g + 1) * POOL_GROUP)
        s = u[:, sl]
        for j in range(POOL_BUF - (w - 1), POOL_BUF):
            s = s + pool_ref[:, j, sl]
        cnt = float(min(PAST_LEN + 1, w))
        diff = s / cnt - u[:, sl]
        yp = _dot(diff.astype(BF16), pw_ref[g]) * ps_ref[:, sl] * _silu(g_pool[:, sl])
        out = out + _dot(yp.astype(BF16), w_out_ref[sl, :])
    pool_o_ref[:, 0:POOL_BUF - 1, :] = pool_ref[:, 1:POOL_BUF, :]
    pool_o_ref[:, POOL_BUF - 1, :] = u

    b_gate = proj(2)
    cv = proj(3) * proj(4)
    conv = cw_ref[0:1, :] * cst_ref[:, 0, :] + cw_ref[1:2, :] * cst_ref[:, 1, :] + cw_ref[2:3, :] * cv
    cst_o_ref[:, 0, :] = cst_ref[:, 1, :]
    cst_o_ref[:, 1, :] = cv
    yc = b_gate * conv * _silu(proj(5))
    out = out + _dot(yc.astype(BF16), w_out_ref[D:2 * D, :])
    x = x + out
    xo_ref[...] = x
    _emit_query(x, gq_ref, wq_ref, q8_ref)


def _sample_call(kern, ops, out_shapes, name):
    args, specs = _operands(ops)
    return pl.pallas_call(
        kern,
        grid=(1,),
        in_specs=specs,
        out_specs=[_const_spec(s) for s in out_shapes],
        out_shape=[jax.ShapeDtypeStruct(s, F32) for s in out_shapes],
        compiler_params=_params(("arbitrary",)),
        name=name,
    )(*args)


def _slab_shape(rows):
    return (rows, D_MODEL // LANES, LANES)


def _even_sample(x, prev, consts, pool, cst, q_consts):
    ops = [x] + (list(prev) if prev else []) + list(consts) + [pool, cst] + list(q_consts)
    kern = functools.partial(_even_sample_kernel, has_prev=prev is not None)
    shapes = [x.shape, _slab_shape(x.shape[0]), pool[0].shape[1:], cst[0].shape[1:]]
    return _sample_call(kern, ops, shapes, "even_sample")


def _xattn_sample_stream_kernel(q_ref, k_ref, v_ref, o_ref, *, bt):
    for b in range(bt):
        r = jnp.sum(k_ref[b] * q_ref[b][None], axis=-1, keepdims=True)
        s = r + pltpu.roll(r, XA_HEADS, 1)
        e = jnp.exp(s - jnp.max(s, axis=0, keepdims=True))
        p = e / jnp.sum(e, axis=0, keepdims=True)
        o_ref[b] = jnp.sum(p * v_ref[b], axis=0)


def _xattn_sample_stream(q8, k8, v8, layer, bt):
    DB = q8.shape[0]
    slab = q8.shape[1:]
    kern = functools.partial(_xattn_sample_stream_kernel, bt=bt)
    return pl.pallas_call(
        kern,
        grid=(DB // bt,),
        in_specs=[pl.BlockSpec((bt,) + slab, lambda i: (i, 0, 0)),
                  pl.BlockSpec((None, bt, N_MEM) + slab, lambda i: (layer, i, 0, 0, 0)),
                  pl.BlockSpec((None, bt, N_MEM) + slab, lambda i: (layer, i, 0, 0, 0))],
        out_specs=pl.BlockSpec((bt,) + slab, lambda i: (i, 0, 0)),
        out_shape=jax.ShapeDtypeStruct(q8.shape, F32),
        compiler_params=_params(("arbitrary",)),
        name="xattn_sample_stream",
    )(q8, k8, v8)


def _xattn_sample_final_kernel(x_ref, o8_ref, wo_ref, gf_ref, xo_ref):
    xo_ref[...] = _rmsnorm(_finish_xattn(x_ref[...], o8_ref, wo_ref), gf_ref[...])


def _xattn_sample_final(x, o8, wo, gf):
    return _sample_call(_xattn_sample_final_kernel, (x, o8, wo, gf), [x.shape], "xattn_sample_final")[0]


def _odd_sample_in_kernel(x_ref, o8_ref, wo_ref, g_ref, wz_ref, wxbc_ref, wdt_ref, cw_ref, cb_ref, dtb_ref, alog_ref,
                          sconv_ref, x1_ref, z_ref, xs_ref, bm_ref, cm_ref, xdt_ref, dec_ref, sconv_o_ref):
    x = _finish_xattn(x_ref[...], o8_ref, wo_ref)
    x1_ref[...] = x
    h = _rmsnorm(x, g_ref[...]).astype(BF16)
    xbc = _dot(h, wxbc_ref[...])
    conv = cw_ref[3:4, :] * xbc
    for k in range(SSM_CONV_K - 1):
        conv = conv + cw_ref[k:k + 1, :] * sconv_ref[:, k, :]
    sconv_o_ref[:, 0:SSM_CONV_K - 2, :] = sconv_ref[:, 1:SSM_CONV_K - 1, :]
    sconv_o_ref[:, SSM_CONV_K - 2, :] = xbc
    xc = _silu(conv + cb_ref[...])
    xs = xc[:, :D_INNER]
    dt = _softplus(_dot(h, wdt_ref[...]) + dtb_ref[...])
    dta = dt * (-jnp.exp(alog_ref[...]))
    hrow = lax.broadcasted_iota(jnp.int32, (LANES, D_INNER), 0)
    ccol = lax.broadcasted_iota(jnp.int32, (LANES, D_INNER), 1)
    expand = jnp.where(ccol // SSM_HEAD_DIM == hrow, 1.0, 0.0).astype(BF16)
    z_ref[...] = _dot(h, wz_ref[...])
    xs_ref[...] = xs
    bm_ref[...] = xc[:, D_INNER:D_INNER + SSM_GROUPS * SSM_STATE]
    cm_ref[...] = xc[:, D_INNER + SSM_GROUPS * SSM_STATE:]
    xdt_ref[...] = xs * _dot_exact_lhs(dt, expand)
    dec_ref[...] = jnp.exp(dta)


def _odd_sample_in(x, prev, consts, sconv):
    DB = x.shape[0]
    ops = [x] + list(prev) + list(consts) + [sconv]
    shapes = [x.shape, (DB, D_INNER), (DB, D_INNER), (DB, SSM_GROUPS * SSM_STATE), (DB, SSM_GROUPS * SSM_STATE),
              (DB, D_INNER), (DB, LANES), sconv[0].shape[1:]]
    return _sample_call(_odd_sample_in_kernel, ops, shapes, "odd_sample_in")


def _odd_sample_state_kernel(dec_ref, xdt_ref, bm_ref, cm_ref, s_ref, *rest, bt):
    so_ref, y_ref = rest[-2:]
    i = pl.program_id(0)
    pad = jnp.zeros((LANES - bt, D_INNER), F32)
    xdt_t = jnp.concatenate([xdt_ref[...], pad], axis=0).T.astype(BF16)
    padn = jnp.zeros((LANES - bt, SSM_STATE), F32)
    row = lax.broadcasted_iota(jnp.int32, (LANES, SSM_STATE), 0)
    y_acc = [jnp.zeros((SSM_GROUP_W, LANES), F32) for _ in range(SSM_GROUPS)]
    heads_per_group = SSM_GROUP_W // SSM_HEAD_DIM
    for g in range(SSM_GROUPS):
        nsl = slice(g * SSM_STATE, (g + 1) * SSM_STATE)
        gsl = slice(g * SSM_GROUP_W, (g + 1) * SSM_GROUP_W)
        b_all = jnp.concatenate([bm_ref[:, nsl], padn], axis=0)
        c_all = jnp.concatenate([cm_ref[:, nsl], padn], axis=0)
        for b in range(bt):
            b_sel = jnp.where(row == b, b_all, 0.0).astype(BF16)
            c_sel = jnp.where(row == b, c_all, 0.0).astype(BF16)
            upd = _dot(xdt_t[gsl, :], b_sel)
            pieces = []
            for e in range(heads_per_group):
                hd = g * heads_per_group + e
                rsl = slice(hd * SSM_HEAD_DIM, (hd + 1) * SSM_HEAD_DIM)
                new = s_ref[b, rsl, :] * dec_ref[i * bt + b, hd] + upd[e * SSM_HEAD_DIM:(e + 1) * SSM_HEAD_DIM, :]
                so_ref[b, rsl, :] = new
                pieces.append(new)
            y_acc[g] = y_acc[g] + _dot_nt(jnp.concatenate(pieces, axis=0).astype(BF16), c_sel)
    y_ref[...] = jnp.concatenate(y_acc, axis=0).T[0:bt, :]


def _odd_sample_state(dec, xdt, bm, cm, state_all, layer, prev_out, bt):
    DB = xdt.shape[0]
    kern = functools.partial(_odd_sample_state_kernel, bt=bt)
    gn = SSM_GROUPS * SSM_STATE
    in_specs = [pl.BlockSpec(memory_space=pltpu.SMEM),
                pl.BlockSpec((bt, D_INNER), lambda i: (i, 0)),
                pl.BlockSpec((bt, gn), lambda i: (i, 0)),
                pl.BlockSpec((bt, gn), lambda i: (i, 0)),
                pl.BlockSpec((None, bt, D_INNER, SSM_STATE), lambda i: (layer, i, 0, 0))]
    args = [dec, xdt, bm, cm, state_all]
    aliases = {}
    if prev_out is not None:
        in_specs.append(pl.BlockSpec(memory_space=pl.ANY))
        args.append(prev_out)
        aliases = {len(args) - 1: 0}
    return pl.pallas_call(
        kern,
        grid=(DB // bt,),
        in_specs=in_specs,
        out_specs=[pl.BlockSpec((None, bt, D_INNER, SSM_STATE), lambda i: (layer, i, 0, 0)),
                   pl.BlockSpec((bt, D_INNER), lambda i: (i, 0))],
        out_shape=[jax.ShapeDtypeStruct(state_all.shape, F32), jax.ShapeDtypeStruct((DB, D_INNER), F32)],
        input_output_aliases=aliases,
        compiler_params=_params(("arbitrary",)),
        name="odd_sample_state",
    )(*args)


def _odd_sample_out_kernel(x_ref, y_ref, xs_ref, z_ref, dskip_ref, nw_ref, w_out_ref, gq_ref, wq_ref,
                           xo_ref, q8_ref):
    yg = (y_ref[...] + dskip_ref[...] * xs_ref[...]) * _silu(z_ref[...])
    out = jnp.zeros(x_ref.shape, F32)
    for g in range(SSM_GROUPS):
        gsl = slice(g * SSM_GROUP_W, (g + 1) * SSM_GROUP_W)
        ygg = yg[:, gsl]
        yn = ygg * lax.rsqrt(jnp.mean(ygg * ygg, axis=-1, keepdims=True) + EPS) * nw_ref[:, gsl]
        out = out + _dot(yn.astype(BF16), w_out_ref[gsl, :])
    x = x_ref[...] + out
    xo_ref[...] = x
    _emit_query(x, gq_ref, wq_ref, q8_ref)


def _odd_sample_out(x, y, xs, z, consts, q_consts):
    ops = [x, y, xs, z] + list(consts) + list(q_consts)
    return _sample_call(_odd_sample_out_kernel, ops, [x.shape, _slab_shape(x.shape[0])], "odd_sample_out")


def _pad_lanes(v):
    return jnp.pad(v, ((0, 0), (0, LANES - v.shape[-1])))


def kernel(x_prompt, x_sample, mem_prompt, cache_mem_k, cache_mem_v, state_pool, state_conv, state_ssm_conv,
           state_ssm, norm_mix, norm_xa, norm_final, w_in_even, pool_w, pool_scale, conv_w, w_out_even, w_in_odd,
           ssm_conv_w, ssm_conv_b, dt_bias, a_log, d_skip, ssm_norm, w_out_odd, w_xa_q, w_xa_k, w_xa_v, w_xa_o):
    B, L, D = x_prompt.shape
    DB = x_sample.shape[0]
    depth = norm_mix.shape[0]
    tile = min(L, 512)
    tile_odd = min(L, 256)
    tile_xa = min(L, 1024)
    bt_x = min(DB, 8)
    bt_s = min(DB, 8)

    mem_k, mem_v, mem_k8, mem_v8 = _kv_proj(mem_prompt.reshape(B * N_MEM, D), w_xa_k, w_xa_v)
    mem_k = mem_k.reshape(depth, B, N_MEM, D)
    mem_v = mem_v.reshape(depth, B, N_MEM, D)
    slab = (D // LANES, LANES)
    cache_k8 = _to_head_slabs(cache_mem_k)
    cache_v8 = _to_head_slabs(cache_mem_v)
    ssm_all = state_ssm.reshape(state_ssm.shape[0], DB, D_INNER, SSM_STATE)
    ssm_s = None

    def rows(a):
        return a.reshape(a.shape[0], 1, a.shape[-1])

    g_mix_all, g_xa_all = rows(norm_mix), rows(norm_xa)
    w_in_e, pw_e, ps_e = w_in_even.astype(BF16), pool_w.astype(BF16), rows(pool_scale)
    w_out_e = w_out_even.astype(BF16)
    wz_o = w_in_odd[:, :, :D_INNER].astype(BF16)
    wxbc_o = w_in_odd[:, :, D_INNER:D_INNER + SSM_CONV_DIM].astype(BF16)
    wdt_o = jnp.pad(w_in_odd[:, :, D_INNER + SSM_CONV_DIM:], ((0, 0), (0, 0), (0, LANES - SSM_HEADS))).astype(BF16)
    cb_o, dtb_o, alog_o = rows(ssm_conv_b), rows(_pad_lanes(dt_bias)), rows(_pad_lanes(a_log))
    dskip_o, nw_o = rows(jnp.repeat(d_skip, SSM_HEAD_DIM, axis=1)), rows(ssm_norm)
    w_out_o = w_out_odd.astype(BF16)
    wq_all, wo_all = w_xa_q.astype(BF16), w_xa_o.astype(BF16)

    xp = x_prompt
    xs_ = x_sample.reshape(DB, D)
    gf = norm_final.reshape(1, D)
    pool_p, pool_s, conv_p, conv_s, sconv_p, sconv_s, ssm_p = [], [], [], [], [], [], []
    prev = None
    for l in range(depth):
        i = l // 2
        g_mix = _layer(g_mix_all, l)
        g_xa, wq, wo = _layer(g_xa_all, l), _layer(wq_all, l), _layer(wo_all, l)
        if l % 2 == 0:
            consts = [g_mix, _layer(w_in_e, i), _layer(pw_e, i), _layer(ps_e, i), _layer(conv_w, i),
                      _layer(w_out_e, i)]
            xp, pb, cbuf = _even_prompt(xp, consts, tile)
            pool_p.append(pb[:, HALO - POOL_BUF:, :])
            conv_p.append(cbuf[:, HALO - (CONV_K - 1):, :])
            xs_, q8, pool_n, cst_n = _even_sample(xs_, prev, consts, _layer(state_pool, i), _layer(state_conv, i),
                                                  (g_xa, wq))
            pool_s.append(pool_n)
            conv_s.append(cst_n)
        else:
            in_consts = [g_mix, _layer(wz_o, i), _layer(wxbc_o, i), _layer(wdt_o, i), _layer(ssm_conv_w, i),
                         _layer(cb_o, i), _layer(dtb_o, i), _layer(alog_o, i)]
            out_consts = [_layer(dskip_o, i), _layer(nw_o, i), _layer(w_out_o, i)]
            xp, sb, st = _odd_prompt(xp, in_consts + out_consts, tile_odd)
            sconv_p.append(sb[:, HALO - (SSM_CONV_K - 1):, :])
            ssm_p.append(st.reshape(B, SSM_HEADS, SSM_HEAD_DIM, SSM_STATE))
            xs_, z, xs_in, bm, cm, xdt, dec, sconv_n = _odd_sample_in(xs_, prev, in_consts,
                                                                      _layer(state_ssm_conv, i))
            ssm_s, y = _odd_sample_state(dec, xdt, bm, cm, ssm_all, i, ssm_s, bt_s)
            xs_, q8 = _odd_sample_out(xs_, y, xs_in, z, out_consts, (g_xa, wq))
            sconv_s.append(sconv_n)
        final = l == depth - 1
        xp = _xattn_prompt(xp, [g_xa, wq, wo], mem_k, mem_v, l, gf, tile_xa, final)
        prev = (_xattn_sample_stream(q8, cache_k8, cache_v8, l, bt_x), wo)
    xs_ = _xattn_sample_final(xs_, prev[0], prev[1], gf)

    mem_shape = (depth, B, N_MEM) + slab
    return (xp, xs_.reshape(DB, 1, D),
            _from_head_slabs(mem_k8.reshape(mem_shape)), _from_head_slabs(mem_v8.reshape(mem_shape)),
            jnp.stack(pool_p), jnp.stack(pool_s), jnp.stack(conv_p), jnp.stack(conv_s),
            jnp.stack(sconv_p), jnp.stack(sconv_s), jnp.stack(ssm_p),
            ssm_s.reshape(ssm_all.shape[0], DB, SSM_HEADS, SSM_HEAD_DIM, SSM_STATE))
```

```python
import functools

import jax
import jax.numpy as jnp
from jax import lax
from jax.experimental import pallas as pl
from jax.experimental.pallas import tpu as pltpu

F32 = jnp.float32
BF16 = jnp.bfloat16

D_MODEL = 1024
PAST_LEN = 16384
POOL_WINDOWS = (2, 4, 8, 16)
POOL_GROUP = D_MODEL // len(POOL_WINDOWS)
POOL_BUF = max(POOL_WINDOWS) - 1
CONV_K = 3
D_INNER = 2 * D_MODEL
SSM_HEAD_DIM = 64
SSM_HEADS = D_INNER // SSM_HEAD_DIM
SSM_GROUPS = 4
SSM_STATE = 128
SSM_GROUP_W = D_INNER // SSM_GROUPS
SSM_CONV_K = 4
SSM_CONV_DIM = D_INNER + 2 * SSM_GROUPS * SSM_STATE
SSM_CHUNK = 128
N_MEM = 256
XA_HEADS = 4
XA_HEAD_DIM = D_MODEL // XA_HEADS
EPS = 1e-6
LOG2E = 1.4426950408889634
LANES = 128
HALO = 16
VMEM_LIMIT = 56 * 1024 * 1024


def _rmsnorm(x, g):
    return x * lax.rsqrt(jnp.mean(x * x, axis=-1, keepdims=True) + EPS) * g


def _silu(x):
    return x * jax.nn.sigmoid(x)


def _softplus(x):
    return jnp.maximum(x, 0.0) + jnp.log1p(jnp.exp(-jnp.abs(x)))


def _dot(a, b):
    return jnp.dot(a, b, preferred_element_type=F32)


def _dot_nt(a, b):
    return lax.dot_general(a, b, (((1,), (1,)), ((), ())), preferred_element_type=F32)


def _split3(x):
    hi = x.astype(BF16)
    r1 = x - hi.astype(F32)
    mid = r1.astype(BF16)
    lo = (r1 - mid.astype(F32)).astype(BF16)
    return hi, mid, lo


def _dot_exact_rhs(a_bf16, x):
    hi, mid, lo = _split3(x)
    return _dot(a_bf16, hi) + _dot(a_bf16, mid) + _dot(a_bf16, lo)


def _dot_exact_lhs(x, b_bf16):
    hi, mid, lo = _split3(x)
    return _dot(hi, b_bf16) + _dot(mid, b_bf16) + _dot(lo, b_bf16)


def _const_spec(shape):
    nd = len(shape)
    return pl.BlockSpec(shape, lambda *_: (0,) * nd, pipeline_mode=pl.Buffered(1))


def _layer(arr, l):
    return (arr, l)


def _operands(ops):
    arrays, specs = [], []
    for op in ops:
        if isinstance(op, tuple):
            arr, l = op
            nd = arr.ndim - 1
            specs.append(pl.BlockSpec((None,) + arr.shape[1:], lambda *_, l=l, nd=nd: (l,) + (0,) * nd,
                                      pipeline_mode=pl.Buffered(1)))
        else:
            arr = op
            specs.append(_const_spec(arr.shape))
        arrays.append(arr)
    return arrays, specs


def _params(sem):
    return pltpu.CompilerParams(dimension_semantics=sem, vmem_limit_bytes=VMEM_LIMIT)


def _head_rows_order():
    lane_tiles = XA_HEAD_DIM // LANES
    return [(r % XA_HEADS) * lane_tiles + r // XA_HEADS for r in range(XA_HEADS * lane_tiles)]


def _to_head_slabs(a):
    lead = a.shape[:-2]
    lane_tiles = XA_HEAD_DIM // LANES
    nd = len(lead)
    a = a.reshape(lead + (XA_HEADS, lane_tiles, LANES))
    a = a.transpose(tuple(range(nd)) + (nd + 1, nd, nd + 2))
    return a.reshape(lead + (XA_HEADS * lane_tiles, LANES))


def _from_head_slabs(a):
    lead = a.shape[:-2]
    lane_tiles = XA_HEAD_DIM // LANES
    nd = len(lead)
    a = a.reshape(lead + (lane_tiles, XA_HEADS, LANES))
    a = a.transpose(tuple(range(nd)) + (nd + 1, nd, nd + 2))
    return a.reshape(lead + (XA_HEADS, XA_HEAD_DIM))


def _kv_proj_kernel(mem_ref, wk_ref, wv_ref, k_ref, v_ref, k8_ref, v8_ref):
    m = mem_ref[...].astype(BF16)
    k = _dot(m, wk_ref[0].astype(BF16))
    v = _dot(m, wv_ref[0].astype(BF16))
    k_ref[0] = k
    v_ref[0] = v
    for r, j in enumerate(_head_rows_order()):
        k8_ref[0, :, r, :] = k[:, j * LANES:(j + 1) * LANES]
        v8_ref[0, :, r, :] = v[:, j * LANES:(j + 1) * LANES]


def _kv_proj(mem2d, w_k, w_v):
    rows = mem2d.shape[0]
    depth = w_k.shape[0]
    tr = min(rows, 512)
    out = jax.ShapeDtypeStruct((depth, rows, D_MODEL), F32)
    out8 = jax.ShapeDtypeStruct((depth, rows, D_MODEL // LANES, LANES), F32)
    return pl.pallas_call(
        _kv_proj_kernel,
        grid=(depth, rows // tr),
        in_specs=[pl.BlockSpec((tr, D_MODEL), lambda l, r: (r, 0)),
                  pl.BlockSpec((1, D_MODEL, D_MODEL), lambda l, r: (l, 0, 0)),
                  pl.BlockSpec((1, D_MODEL, D_MODEL), lambda l, r: (l, 0, 0))],
        out_specs=[pl.BlockSpec((1, tr, D_MODEL), lambda l, r: (l, r, 0)),
                   pl.BlockSpec((1, tr, D_MODEL), lambda l, r: (l, r, 0)),
                   pl.BlockSpec((1, tr, D_MODEL // LANES, LANES), lambda l, r: (l, r, 0, 0)),
                   pl.BlockSpec((1, tr, D_MODEL // LANES, LANES), lambda l, r: (l, r, 0, 0))],
        out_shape=[out, out, out8, out8],
        compiler_params=_params(("arbitrary", "arbitrary")),
        name="kv_proj",
    )(mem2d, w_k, w_v)


def _even_prompt_kernel(x_ref, g_ref, w_in_ref, pw_ref, ps_ref, cw_ref, w_out_ref,
                        xo_ref, pool_ref, conv_ref, ext_ref, cext_ref, *, tile):
    t = pl.program_id(1)
    D = D_MODEL

    @pl.when(t == 0)
    def _():
        ext_ref[0:HALO, :] = jnp.zeros((HALO, D), F32)
        cext_ref[0:HALO, :] = jnp.zeros((HALO, D), F32)

    @pl.when(t > 0)
    def _():
        ext_ref[0:HALO, :] = ext_ref[tile:tile + HALO, :]
        cext_ref[0:HALO, :] = cext_ref[tile:tile + HALO, :]

    x = x_ref[0]
    h = _rmsnorm(x, g_ref[...]).astype(BF16)

    def proj(j):
        return _dot(h, w_in_ref[:, j * D:(j + 1) * D])

    u = proj(0)
    ext_ref[HALO:HALO + tile, :] = u
    g_pool = proj(1)
    pos = t * tile + lax.broadcasted_iota(jnp.int32, (tile, POOL_GROUP), 0)
    out = jnp.zeros((tile, D), F32)
    for g, w in enumerate(POOL_WINDOWS):
        sl = slice(g * POOL_GROUP, (g + 1) * POOL_GROUP)
        s = ext_ref[:, sl]
        k = 1
        while k < w:
            s = s + pltpu.roll(s, k, 0)
            k *= 2
        cnt = jnp.minimum(pos + 1, w).astype(F32)
        diff = s[HALO:, :] / cnt - u[:, sl]
        yp = _dot(diff.astype(BF16), pw_ref[g]) * ps_ref[:, sl] * _silu(g_pool[:, sl])
        out = out + _dot(yp.astype(BF16), w_out_ref[sl, :])

    b_gate = proj(2)
    cv = proj(3) * proj(4)
    cext_ref[HALO:HALO + tile, :] = cv
    ce = cext_ref[...]
    conv = cw_ref[2:3, :] * cv
    conv = conv + cw_ref[1:2, :] * pltpu.roll(ce, 1, 0)[HALO:, :]
    conv = conv + cw_ref[0:1, :] * pltpu.roll(ce, 2, 0)[HALO:, :]
    yc = b_gate * conv * _silu(proj(5))
    out = out + _dot(yc.astype(BF16), w_out_ref[D:2 * D, :])
    xo_ref[0] = x + out

    @pl.when(t == pl.num_programs(1) - 1)
    def _():
        pool_ref[0] = ext_ref[tile:tile + HALO, :]
        conv_ref[0] = cext_ref[tile:tile + HALO, :]


def _even_prompt(x, consts, tile):
    B, L, D = x.shape
    kern = functools.partial(_even_prompt_kernel, tile=tile)
    arrays, specs = _operands(consts)
    return pl.pallas_call(
        kern,
        grid=(B, L // tile),
        in_specs=[pl.BlockSpec((1, tile, D), lambda b, t: (b, t, 0))] + specs,
        out_specs=[pl.BlockSpec((1, tile, D), lambda b, t: (b, t, 0)),
                   pl.BlockSpec((1, HALO, D), lambda b, t: (b, 0, 0)),
                   pl.BlockSpec((1, HALO, D), lambda b, t: (b, 0, 0))],
        out_shape=[jax.ShapeDtypeStruct((B, L, D), F32),
                   jax.ShapeDtypeStruct((B, HALO, D), F32),
                   jax.ShapeDtypeStruct((B, HALO, D), F32)],
        scratch_shapes=[pltpu.VMEM((HALO + tile, D), F32), pltpu.VMEM((HALO + tile, D), F32)],
        compiler_params=_params(("arbitrary", "arbitrary")),
        name="even_prompt",
    )(x, *arrays)


def _xattn_prompt_kernel(x_ref, g_ref, wq_ref, wo_ref, k_ref, v_ref, gf_ref, xo_ref, *, final):
    x = x_ref[0]
    h = _rmsnorm(x, g_ref[...]).astype(BF16)
    q = (_dot(h, wq_ref[...]) * (XA_HEAD_DIM ** -0.5)).astype(BF16)
    out = jnp.zeros_like(x)
    for hd in range(XA_HEADS):
        sl = slice(hd * XA_HEAD_DIM, (hd + 1) * XA_HEAD_DIM)
        kh = k_ref[0, :, sl].astype(BF16)
        vh = v_ref[0, :, sl].astype(BF16)
        s = _dot_nt(q[:, sl], kh)
        e = jnp.exp(s - jnp.max(s, axis=-1, keepdims=True))
        o = _dot(e.astype(BF16), vh) / jnp.sum(e, axis=-1, keepdims=True)
        out = out + _dot(o.astype(BF16), wo_ref[sl, :])
    y = x + out
    if final:
        y = _rmsnorm(y, gf_ref[...])
    xo_ref[0] = y


def _xattn_prompt(x, consts, k, v, layer, gf, tile, final):
    B, L, D = x.shape
    kern = functools.partial(_xattn_prompt_kernel, final=final)
    arrays, specs = _operands(consts)
    return pl.pallas_call(
        kern,
        grid=(B, L // tile),
        in_specs=[pl.BlockSpec((1, tile, D), lambda b, t: (b, t, 0))] + specs + [
            pl.BlockSpec((None, 1, N_MEM, D), lambda b, t: (layer, b, 0, 0)),
            pl.BlockSpec((None, 1, N_MEM, D), lambda b, t: (layer, b, 0, 0)),
            _const_spec(gf.shape)],
        out_specs=pl.BlockSpec((1, tile, D), lambda b, t: (b, t, 0)),
        out_shape=jax.ShapeDtypeStruct((B, L, D), F32),
        compiler_params=_params(("arbitrary", "arbitrary")),
        name="xattn_prompt",
    )(x, *arrays, k, v, gf)


def _odd_prompt_kernel(x_ref, g_ref, wz_ref, wxbc_ref, wdt_ref, cw_ref, cb_ref, dtb_ref, alog_ref,
                       dskip_ref, nw_ref, w_out_ref,
                       xo_ref, sconv_ref, state_ref, xext_ref, st_ref, y_ref, *, tile):
    t = pl.program_id(1)
    Q = SSM_CHUNK
    C = SSM_CONV_DIM

    @pl.when(t == 0)
    def _():
        xext_ref[0:HALO, :] = jnp.zeros((HALO, C), F32)
        st_ref[...] = jnp.zeros(st_ref.shape, F32)

    @pl.when(t > 0)
    def _():
        xext_ref[0:HALO, :] = xext_ref[tile:tile + HALO, :]

    x = x_ref[0]
    h = _rmsnorm(x, g_ref[...]).astype(BF16)
    xbc = _dot(h, wxbc_ref[...])
    xext_ref[HALO:HALO + tile, :] = xbc
    xe = xext_ref[...]
    conv = cw_ref[3:4, :] * xbc
    for k in range(1, SSM_CONV_K):
        conv = conv + cw_ref[3 - k:4 - k, :] * pltpu.roll(xe, k, 0)[HALO:, :]
    xc = _silu(conv + cb_ref[...])
    dt = _softplus(_dot(h, wdt_ref[...]) + dtb_ref[...])
    dta = dt * (-jnp.exp(alog_ref[...]) * LOG2E)

    row = lax.broadcasted_iota(jnp.int32, (Q, Q), 0)
    col = lax.broadcasted_iota(jnp.int32, (Q, Q), 1)
    causal = row >= col
    tri = jnp.where(causal, 1.0, 0.0).astype(BF16)
    lo_lane = lax.broadcasted_iota(jnp.int32, (1, LANES), 1) < SSM_HEAD_DIM
    neg_inf = jnp.float32(-jnp.inf)

    for c in range(tile // Q):
        rs = slice(c * Q, (c + 1) * Q)
        a_cs = _dot_exact_rhs(tri, dta[rs, :])
        a_cs_t = a_cs.T
        dt_t = dt[rs, :].T
        w_t = dt_t * jnp.exp2(a_cs_t[:, Q - 1:Q] - a_cs_t)
        ea = jnp.exp2(a_cs)
        dec = ea[Q - 1:Q, :]
        for g in range(SSM_GROUPS):
            bsl = slice(D_INNER + g * SSM_STATE, D_INNER + (g + 1) * SSM_STATE)
            csl = slice(D_INNER + SSM_GROUPS * SSM_STATE + g * SSM_STATE,
                        D_INNER + SSM_GROUPS * SSM_STATE + (g + 1) * SSM_STATE)
            b_t = xc[rs, bsl].T
            c_g = xc[rs, csl].astype(BF16)
            cb = _dot(c_g, b_t.astype(BF16))
            gsl = slice(g * SSM_GROUP_W, (g + 1) * SSM_GROUP_W)
            y_off = _dot(c_g, st_ref[:, gsl].astype(BF16))
            for j in range(SSM_GROUP_W // LANES):
                h0 = (g * SSM_GROUP_W + j * LANES) // SSM_HEAD_DIM
                lsl = slice(g * SSM_GROUP_W + j * LANES, g * SSM_GROUP_W + (j + 1) * LANES)
                xs = xc[rs, lsl]
                wmat = jnp.concatenate([jnp.where(lo_lane, xs, 0.0), jnp.where(lo_lane, 0.0, xs)],
                                       axis=0).astype(BF16)
                tops, bots = [], []
                for hh in (h0, h0 + 1):
                    seg = a_cs[:, hh:hh + 1] - a_cs_t[hh:hh + 1, :]
                    decay = jnp.exp2(jnp.where(causal, seg, neg_inf))
                    tops.append(cb * decay * dt_t[hh:hh + 1, :])
                    bots.append(b_t * w_t[hh:hh + 1, :])
                lhs = jnp.concatenate([jnp.concatenate(tops, axis=1), jnp.concatenate(bots, axis=1)],
                                      axis=0).astype(BF16)
                r = _dot(lhs, wmat)
                escale = jnp.where(lo_lane, ea[:, h0:h0 + 1], ea[:, h0 + 1:h0 + 2])
                y_ref[rs, lsl] = (r[0:Q, :] + y_off[:, j * LANES:(j + 1) * LANES] * escale
                                  + dskip_ref[:, lsl] * xs)
                dscale = jnp.where(lo_lane, dec[:, h0:h0 + 1], dec[:, h0 + 1:h0 + 2])
                st_ref[:, lsl] = st_ref[:, lsl] * dscale + r[Q:2 * Q, :]

    yg = y_ref[...] * _silu(_dot(h, wz_ref[...]))
    out = jnp.zeros((tile, D_MODEL), F32)
    for g in range(SSM_GROUPS):
        gsl = slice(g * SSM_GROUP_W, (g + 1) * SSM_GROUP_W)
        ygg = yg[:, gsl]
        yn = ygg * lax.rsqrt(jnp.mean(ygg * ygg, axis=-1, keepdims=True) + EPS) * nw_ref[:, gsl]
        out = out + _dot(yn.astype(BF16), w_out_ref[gsl, :])
    xo_ref[0] = x + out

    @pl.when(t == pl.num_programs(1) - 1)
    def _():
        sconv_ref[0] = xext_ref[tile:tile + HALO, :]
        state_ref[0] = st_ref[...].T


def _odd_prompt(x, consts, tile):
    B, L, D = x.shape
    kern = functools.partial(_odd_prompt_kernel, tile=tile)
    arrays, specs = _operands(consts)
    return pl.pallas_call(
        kern,
        grid=(B, L // tile),
        in_specs=[pl.BlockSpec((1, tile, D), lambda b, t: (b, t, 0))] + specs,
        out_specs=[pl.BlockSpec((1, tile, D), lambda b, t: (b, t, 0)),
                   pl.BlockSpec((1, HALO, SSM_CONV_DIM), lambda b, t: (b, 0, 0)),
                   pl.BlockSpec((1, D_INNER, SSM_STATE), lambda b, t: (b, 0, 0))],
        out_shape=[jax.ShapeDtypeStruct((B, L, D), F32),
                   jax.ShapeDtypeStruct((B, HALO, SSM_CONV_DIM), F32),
                   jax.ShapeDtypeStruct((B, D_INNER, SSM_STATE), F32)],
        scratch_shapes=[pltpu.VMEM((HALO + tile, SSM_CONV_DIM), F32),
                        pltpu.VMEM((SSM_STATE, D_INNER), F32),
                        pltpu.VMEM((tile, D_INNER), F32)],
        compiler_params=_params(("arbitrary", "arbitrary")),
        name="odd_prompt",
    )(x, *arrays)


def _finish_xattn(x, o8_ref, wo_ref):
    for r, j in enumerate(_head_rows_order()):
        x = x + _dot(o8_ref[:, r, :].astype(BF16), wo_ref[j * LANES:(j + 1) * LANES, :])
    return x


def _emit_query(x, g_ref, wq_ref, q8_ref):
    q = _dot(_rmsnorm(x, g_ref[...]).astype(BF16), wq_ref[...]) * (XA_HEAD_DIM ** -0.5)
    for r, j in enumerate(_head_rows_order()):
        q8_ref[:, r, :] = q[:, j * LANES:(j + 1) * LANES]


def _even_sample_kernel(*refs, has_prev):
    refs = list(refs)
    x_ref = refs.pop(0)
    x = x_ref[...]
    if has_prev:
        o8_ref, wo_ref = refs.pop(0), refs.pop(0)
        x = _finish_xattn(x, o8_ref, wo_ref)
    (g_ref, w_in_ref, pw_ref, ps_ref, cw_ref, w_out_ref, pool_ref, cst_ref, gq_ref, wq_ref,
     xo_ref, q8_ref, pool_o_ref, cst_o_ref) = refs
    D = D_MODEL
    h = _rmsnorm(x, g_ref[...]).astype(BF16)

    def proj(j):
        return _dot(h, w_in_ref[:, j * D:(j + 1) * D])

    u = proj(0)
    g_pool = proj(1)
    out = jnp.zeros_like(x)
    for g, w in enumerate(POOL_WINDOWS):
        sl = slice(g * POOL_GROUP, (g + 1) * POOL_GROUP)
        s = u[:, sl]
        for j in range(POOL_BUF - (w - 1), POOL_BUF):
            s = s + pool_ref[j, :, sl]
        cnt = float(min(PAST_LEN + 1, w))
        diff = s / cnt - u[:, sl]
        yp = _dot(diff.astype(BF16), pw_ref[g]) * ps_ref[:, sl] * _silu(g_pool[:, sl])
        out = out + _dot(yp.astype(BF16), w_out_ref[sl, :])
    for j in range(POOL_BUF - 1):
        pool_o_ref[j] = pool_ref[j + 1]
    pool_o_ref[POOL_BUF - 1] = u

    b_gate = proj(2)
    cv = proj(3) * proj(4)
    conv = cw_ref[0:1, :] * cst_ref[:, 0, :] + cw_ref[1:2, :] * cst_ref[:, 1, :] + cw_ref[2:3, :] * cv
    cst_o_ref[:, 0, :] = cst_ref[:, 1, :]
    cst_o_ref[:, 1, :] = cv
    yc = b_gate * conv * _silu(proj(5))
    out = out + _dot(yc.astype(BF16), w_out_ref[D:2 * D, :])
    x = x + out
    xo_ref[...] = x
    _emit_query(x, gq_ref, wq_ref, q8_ref)


class _StackedOut:
    def __init__(self, shape, layer, prev):
        self.shape, self.layer, self.prev = shape, layer, prev


def _sample_call(kern, ops, outs, name):
    args, specs = _operands(ops)
    n_in = len(args)
    out_specs, out_shapes, aliases = [], [], {}
    for k, o in enumerate(outs):
        if isinstance(o, _StackedOut):
            nd = len(o.shape) - 1
            out_specs.append(pl.BlockSpec((None,) + o.shape[1:], lambda *_, l=o.layer, nd=nd: (l,) + (0,) * nd))
            out_shapes.append(jax.ShapeDtypeStruct(o.shape, F32))
            if o.prev is not None:
                aliases[len(args)] = k
                args.append(o.prev)
                specs.append(pl.BlockSpec(memory_space=pl.ANY))
        else:
            out_specs.append(_const_spec(o))
            out_shapes.append(jax.ShapeDtypeStruct(o, F32))
    n_alias = len(args) - n_in

    def body(*refs):
        kern(*refs[:n_in], *refs[n_in + n_alias:])

    return pl.pallas_call(
        body,
        grid=(1,),
        in_specs=specs,
        out_specs=out_specs,
        out_shape=out_shapes,
        input_output_aliases=aliases,
        compiler_params=_params(("arbitrary",)),
        name=name,
    )(*args)


def _slab_shape(rows):
    return (rows, D_MODEL // LANES, LANES)


def _even_sample(x, prev, consts, pool, pool_out, cst, q_consts):
    ops = [x] + (list(prev) if prev else []) + list(consts) + [pool, cst] + list(q_consts)
    kern = functools.partial(_even_sample_kernel, has_prev=prev is not None)
    outs = [x.shape, _slab_shape(x.shape[0]), _StackedOut(pool[0].shape, pool[1], pool_out), cst[0].shape[1:]]
    return _sample_call(kern, ops, outs, "even_sample")


def _xattn_sample_stream_kernel(q_ref, k_ref, v_ref, o_ref, *, bt):
    for b in range(bt):
        r = jnp.sum(k_ref[b] * q_ref[b][None], axis=-1, keepdims=True)
        s = r + pltpu.roll(r, XA_HEADS, 1)
        e = jnp.exp(s - jnp.max(s, axis=0, keepdims=True))
        o_ref[b] = jnp.sum(e * v_ref[b], axis=0) / jnp.sum(e, axis=0)


def _xattn_sample_stream(q8, k8, v8, layer, bt):
    DB = q8.shape[0]
    slab = q8.shape[1:]
    kern = functools.partial(_xattn_sample_stream_kernel, bt=bt)
    return pl.pallas_call(
        kern,
        grid=(DB // bt,),
        in_specs=[pl.BlockSpec((bt,) + slab, lambda i: (i, 0, 0)),
                  pl.BlockSpec((None, bt, N_MEM) + slab, lambda i: (layer, i, 0, 0, 0)),
                  pl.BlockSpec((None, bt, N_MEM) + slab, lambda i: (layer, i, 0, 0, 0))],
        out_specs=pl.BlockSpec((bt,) + slab, lambda i: (i, 0, 0)),
        out_shape=jax.ShapeDtypeStruct(q8.shape, F32),
        compiler_params=_params(("arbitrary",)),
        name="xattn_sample_stream",
    )(q8, k8, v8)


def _xattn_sample_final_kernel(x_ref, o8_ref, wo_ref, gf_ref, xo_ref):
    xo_ref[...] = _rmsnorm(_finish_xattn(x_ref[...], o8_ref, wo_ref), gf_ref[...])


def _xattn_sample_final(x, o8, wo, gf):
    return _sample_call(_xattn_sample_final_kernel, (x, o8, wo, gf), [x.shape], "xattn_sample_final")[0]


def _odd_sample_in_kernel(x_ref, o8_ref, wo_ref, g_ref, wz_ref, wxbc_ref, wdt_ref, cw_ref, cb_ref, dtb_ref, alog_ref,
                          sconv_ref, x1_ref, z_ref, xs_ref, bm_ref, cm_ref, xdt_ref, dec_ref, sconv_o_ref):
    x = _finish_xattn(x_ref[...], o8_ref, wo_ref)
    x1_ref[...] = x
    h = _rmsnorm(x, g_ref[...]).astype(BF16)
    xbc = _dot(h, wxbc_ref[...])
    conv = cw_ref[3:4, :] * xbc
    for k in range(SSM_CONV_K - 1):
        conv = conv + cw_ref[k:k + 1, :] * sconv_ref[k]
    for k in range(SSM_CONV_K - 2):
        sconv_o_ref[k] = sconv_ref[k + 1]
    sconv_o_ref[SSM_CONV_K - 2] = xbc
    xc = _silu(conv + cb_ref[...])
    xs = xc[:, :D_INNER]
    dt = _softplus(_dot(h, wdt_ref[...]) + dtb_ref[...])
    dta = dt * (-jnp.exp(alog_ref[...]))
    hrow = lax.broadcasted_iota(jnp.int32, (LANES, D_INNER), 0)
    ccol = lax.broadcasted_iota(jnp.int32, (LANES, D_INNER), 1)
    expand = jnp.where(ccol // SSM_HEAD_DIM == hrow, 1.0, 0.0).astype(BF16)
    z_ref[...] = _dot(h, wz_ref[...])
    xs_ref[...] = xs
    bm_ref[...] = xc[:, D_INNER:D_INNER + SSM_GROUPS * SSM_STATE]
    cm_ref[...] = xc[:, D_INNER + SSM_GROUPS * SSM_STATE:]
    xdt_ref[...] = xs * _dot_exact_lhs(dt, expand)
    dec_ref[...] = jnp.exp(dta)


def _odd_sample_in(x, prev, consts, sconv, sconv_out):
    DB = x.shape[0]
    ops = [x] + list(prev) + list(consts) + [sconv]
    outs = [x.shape, (DB, D_INNER), (DB, D_INNER), (DB, SSM_GROUPS * SSM_STATE), (DB, SSM_GROUPS * SSM_STATE),
            (DB, D_INNER), (DB, LANES), _StackedOut(sconv[0].shape, sconv[1], sconv_out)]
    return _sample_call(_odd_sample_in_kernel, ops, outs, "odd_sample_in")


def _odd_sample_state_kernel(dec_ref, xdt_ref, bm_ref, cm_ref, s_ref, *rest, bt):
    so_ref, y_ref = rest[-2:]
    i = pl.program_id(0)
    pad = jnp.zeros((LANES - bt, D_INNER), F32)
    xdt_t = jnp.concatenate([xdt_ref[...], pad], axis=0).T.astype(BF16)
    padn = jnp.zeros((LANES - bt, SSM_STATE), F32)
    row = lax.broadcasted_iota(jnp.int32, (LANES, SSM_STATE), 0)
    y_acc = [jnp.zeros((SSM_GROUP_W, LANES), F32) for _ in range(SSM_GROUPS)]
    heads_per_group = SSM_GROUP_W // SSM_HEAD_DIM
    for g in range(SSM_GROUPS):
        nsl = slice(g * SSM_STATE, (g + 1) * SSM_STATE)
        gsl = slice(g * SSM_GROUP_W, (g + 1) * SSM_GROUP_W)
        b_all = jnp.concatenate([bm_ref[:, nsl], padn], axis=0)
        c_all = jnp.concatenate([cm_ref[:, nsl], padn], axis=0)
        for b in range(bt):
            b_sel = jnp.where(row == b, b_all, 0.0).astype(BF16)
            c_sel = jnp.where(row == b, c_all, 0.0).astype(BF16)
            upd = _dot(xdt_t[gsl, :], b_sel)
            pieces = []
            for e in range(heads_per_group):
                hd = g * heads_per_group + e
                rsl = slice(hd * SSM_HEAD_DIM, (hd + 1) * SSM_HEAD_DIM)
                new = s_ref[b, rsl, :] * dec_ref[i * bt + b, hd] + upd[e * SSM_HEAD_DIM:(e + 1) * SSM_HEAD_DIM, :]
                so_ref[b, rsl, :] = new
                pieces.append(new)
            y_acc[g] = y_acc[g] + _dot_nt(jnp.concatenate(pieces, axis=0).astype(BF16), c_sel)
    y_ref[...] = jnp.concatenate(y_acc, axis=0).T[0:bt, :]


def _odd_sample_state(dec, xdt, bm, cm, state_all, layer, prev_out, bt):
    DB = xdt.shape[0]
    kern = functools.partial(_odd_sample_state_kernel, bt=bt)
    gn = SSM_GROUPS * SSM_STATE
    in_specs = [pl.BlockSpec(memory_space=pltpu.SMEM),
                pl.BlockSpec((bt, D_INNER), lambda i: (i, 0)),
                pl.BlockSpec((bt, gn), lambda i: (i, 0)),
                pl.BlockSpec((bt, gn), lambda i: (i, 0)),
                pl.BlockSpec((None, bt, D_INNER, SSM_STATE), lambda i: (layer, i, 0, 0))]
    args = [dec, xdt, bm, cm, state_all]
    aliases = {}
    if prev_out is not None:
        in_specs.append(pl.BlockSpec(memory_space=pl.ANY))
        args.append(prev_out)
        aliases = {len(args) - 1: 0}
    return pl.pallas_call(
        kern,
        grid=(DB // bt,),
        in_specs=in_specs,
        out_specs=[pl.BlockSpec((None, bt, D_INNER, SSM_STATE), lambda i: (layer, i, 0, 0)),
                   pl.BlockSpec((bt, D_INNER), lambda i: (i, 0))],
        out_shape=[jax.ShapeDtypeStruct(state_all.shape, F32), jax.ShapeDtypeStruct((DB, D_INNER), F32)],
        input_output_aliases=aliases,
        compiler_params=_params(("arbitrary",)),
        name="odd_sample_state",
    )(*args)


def _odd_sample_out_kernel(x_ref, y_ref, xs_ref, z_ref, dskip_ref, nw_ref, w_out_ref, gq_ref, wq_ref,
                           xo_ref, q8_ref):
    yg = (y_ref[...] + dskip_ref[...] * xs_ref[...]) * _silu(z_ref[...])
    out = jnp.zeros(x_ref.shape, F32)
    for g in range(SSM_GROUPS):
        gsl = slice(g * SSM_GROUP_W, (g + 1) * SSM_GROUP_W)
        ygg = yg[:, gsl]
        yn = ygg * lax.rsqrt(jnp.mean(ygg * ygg, axis=-1, keepdims=True) + EPS) * nw_ref[:, gsl]
        out = out + _dot(yn.astype(BF16), w_out_ref[gsl, :])
    x = x_ref[...] + out
    xo_ref[...] = x
    _emit_query(x, gq_ref, wq_ref, q8_ref)


def _odd_sample_out(x, y, xs, z, consts, q_consts):
    ops = [x, y, xs, z] + list(consts) + list(q_consts)
    return _sample_call(_odd_sample_out_kernel, ops, [x.shape, _slab_shape(x.shape[0])], "odd_sample_out")


def _pad_lanes(v):
    return jnp.pad(v, ((0, 0), (0, LANES - v.shape[-1])))


def kernel(x_prompt, x_sample, mem_prompt, cache_mem_k, cache_mem_v, state_pool, state_conv, state_ssm_conv,
           state_ssm, norm_mix, norm_xa, norm_final, w_in_even, pool_w, pool_scale, conv_w, w_out_even, w_in_odd,
           ssm_conv_w, ssm_conv_b, dt_bias, a_log, d_skip, ssm_norm, w_out_odd, w_xa_q, w_xa_k, w_xa_v, w_xa_o):
    B, L, D = x_prompt.shape
    DB = x_sample.shape[0]
    depth = norm_mix.shape[0]
    tile = min(L, 512)
    tile_odd = min(L, 256)
    tile_xa = min(L, 1024)
    bt_x = min(DB, 8)
    bt_s = min(DB, 8)

    mem_k, mem_v, mem_k8, mem_v8 = _kv_proj(mem_prompt.reshape(B * N_MEM, D), w_xa_k, w_xa_v)
    mem_k = mem_k.reshape(depth, B, N_MEM, D)
    mem_v = mem_v.reshape(depth, B, N_MEM, D)
    slab = (D // LANES, LANES)
    cache_k8 = _to_head_slabs(cache_mem_k)
    cache_v8 = _to_head_slabs(cache_mem_v)
    ssm_all = state_ssm.reshape(state_ssm.shape[0], DB, D_INNER, SSM_STATE)
    ssm_s = None

    def rows(a):
        return a.reshape(a.shape[0], 1, a.shape[-1])

    g_mix_all, g_xa_all = rows(norm_mix), rows(norm_xa)
    w_in_e, pw_e, ps_e = w_in_even.astype(BF16), pool_w.astype(BF16), rows(pool_scale)
    w_out_e = w_out_even.astype(BF16)
    w_in_o = w_in_odd.astype(BF16)
    wz_o = w_in_o[:, :, :D_INNER]
    wxbc_o = w_in_o[:, :, D_INNER:D_INNER + SSM_CONV_DIM]
    wdt_o = jnp.pad(w_in_o[:, :, D_INNER + SSM_CONV_DIM:], ((0, 0), (0, 0), (0, LANES - SSM_HEADS)))
    cb_o, dtb_o, alog_o = rows(ssm_conv_b), rows(_pad_lanes(dt_bias)), rows(_pad_lanes(a_log))
    dskip_o, nw_o = rows(jnp.repeat(d_skip, SSM_HEAD_DIM, axis=1)), rows(ssm_norm)
    w_out_o = w_out_odd.astype(BF16)
    wq_all, wo_all = w_xa_q.astype(BF16), w_xa_o.astype(BF16)

    xp = x_prompt
    xs_ = x_sample.reshape(DB, D)
    gf = norm_final.reshape(1, D)
    pool_p, conv_p, conv_s, sconv_p, ssm_p = [], [], [], [], []
    pool_t, sconv_t = state_pool.transpose(0, 2, 1, 3), state_ssm_conv.transpose(0, 2, 1, 3)
    pool_s = sconv_s = None
    prev = None
    for l in range(depth):
        i = l // 2
        g_mix = _layer(g_mix_all, l)
        g_xa, wq, wo = _layer(g_xa_all, l), _layer(wq_all, l), _layer(wo_all, l)
        if l % 2 == 0:
            consts = [g_mix, _layer(w_in_e, i), _layer(pw_e, i), _layer(ps_e, i), _layer(conv_w, i),
                      _layer(w_out_e, i)]
            xp, pb, cbuf = _even_prompt(xp, consts, tile)
            pool_p.append(pb[:, HALO - POOL_BUF:, :])
            conv_p.append(cbuf[:, HALO - (CONV_K - 1):, :])
            xs_, q8, pool_s, cst_n = _even_sample(xs_, prev, consts, _layer(pool_t, i), pool_s,
                                                  _layer(state_conv, i), (g_xa, wq))
            conv_s.append(cst_n)
        else:
            in_consts = [g_mix, _layer(wz_o, i), _layer(wxbc_o, i), _layer(wdt_o, i), _layer(ssm_conv_w, i),
                         _layer(cb_o, i), _layer(dtb_o, i), _layer(alog_o, i)]
            out_consts = [_layer(dskip_o, i), _layer(nw_o, i), _layer(w_out_o, i)]
            xp, sb, st = _odd_prompt(xp, in_consts + out_consts, tile_odd)
            sconv_p.append(sb[:, HALO - (SSM_CONV_K - 1):, :])
            ssm_p.append(st.reshape(B, SSM_HEADS, SSM_HEAD_DIM, SSM_STATE))
            xs_, z, xs_in, bm, cm, xdt, dec, sconv_s = _odd_sample_in(xs_, prev, in_consts,
                                                                      _layer(sconv_t, i), sconv_s)
            ssm_s, y = _odd_sample_state(dec, xdt, bm, cm, ssm_all, i, ssm_s, bt_s)
            xs_, q8 = _odd_sample_out(xs_, y, xs_in, z, out_consts, (g_xa, wq))
        final = l == depth - 1
        xp = _xattn_prompt(xp, [g_xa, wq, wo], mem_k, mem_v, l, gf, tile_xa, final)
        prev = (_xattn_sample_stream(q8, cache_k8, cache_v8, l, bt_x), wo)
    xs_ = _xattn_sample_final(xs_, prev[0], prev[1], gf)

    mem_shape = (depth, B, N_MEM) + slab
    return (xp, xs_.reshape(DB, 1, D),
            _from_head_slabs(mem_k8.reshape(mem_shape)), _from_head_slabs(mem_v8.reshape(mem_shape)),
            jnp.stack(pool_p), pool_s.transpose(0, 2, 1, 3), jnp.stack(conv_p), jnp.stack(conv_s),
            jnp.stack(sconv_p), sconv_s.transpose(0, 2, 1, 3), jnp.stack(ssm_p),
            ssm_s.reshape(ssm_all.shape[0], DB, SSM_HEADS, SSM_HEAD_DIM, SSM_STATE))
```

```python
import functools

import jax
import jax.numpy as jnp
from jax import lax
from jax.experimental import pallas as pl
from jax.experimental.pallas import tpu as pltpu

F32 = jnp.float32
BF16 = jnp.bfloat16

D_MODEL = 1024
PAST_LEN = 16384
POOL_WINDOWS = (2, 4, 8, 16)
POOL_GROUP = D_MODEL // len(POOL_WINDOWS)
POOL_BUF = max(POOL_WINDOWS) - 1
CONV_K = 3
D_INNER = 2 * D_MODEL
SSM_HEAD_DIM = 64
SSM_HEADS = D_INNER // SSM_HEAD_DIM
SSM_GROUPS = 4
SSM_STATE = 128
SSM_GROUP_W = D_INNER // SSM_GROUPS
SSM_CONV_K = 4
SSM_CONV_DIM = D_INNER + 2 * SSM_GROUPS * SSM_STATE
SSM_CHUNK = 128
N_MEM = 256
XA_HEADS = 4
XA_HEAD_DIM = D_MODEL // XA_HEADS
EPS = 1e-6
LOG2E = 1.4426950408889634
LANES = 128
HALO = 16
VMEM_LIMIT = 56 * 1024 * 1024


def _rmsnorm(x, g):
    return x * lax.rsqrt(jnp.mean(x * x, axis=-1, keepdims=True) + EPS) * g


def _silu(x):
    return x * jax.nn.sigmoid(x)


def _softplus(x):
    return jnp.maximum(x, 0.0) + jnp.log1p(jnp.exp(-jnp.abs(x)))


def _dot(a, b):
    return jnp.dot(a, b, preferred_element_type=F32)


def _dot_nt(a, b):
    return lax.dot_general(a, b, (((1,), (1,)), ((), ())), preferred_element_type=F32)


def _split3(x):
    hi = x.astype(BF16)
    r1 = x - hi.astype(F32)
    mid = r1.astype(BF16)
    lo = (r1 - mid.astype(F32)).astype(BF16)
    return hi, mid, lo


def _dot_exact_rhs(a_bf16, x):
    hi, mid, lo = _split3(x)
    return _dot(a_bf16, hi) + _dot(a_bf16, mid) + _dot(a_bf16, lo)


def _dot_exact_lhs(x, b_bf16):
    hi, mid, lo = _split3(x)
    return _dot(hi, b_bf16) + _dot(mid, b_bf16) + _dot(lo, b_bf16)


def _const_spec(shape):
    nd = len(shape)
    return pl.BlockSpec(shape, lambda *_: (0,) * nd, pipeline_mode=pl.Buffered(1))


def _layer(arr, l):
    return (arr, l)


def _operands(ops):
    arrays, specs = [], []
    for op in ops:
        if isinstance(op, tuple):
            arr, l = op
            nd = arr.ndim - 1
            specs.append(pl.BlockSpec((None,) + arr.shape[1:], lambda *_, l=l, nd=nd: (l,) + (0,) * nd,
                                      pipeline_mode=pl.Buffered(1)))
        else:
            arr = op
            specs.append(_const_spec(arr.shape))
        arrays.append(arr)
    return arrays, specs


def _params(sem):
    return pltpu.CompilerParams(dimension_semantics=sem, vmem_limit_bytes=VMEM_LIMIT)


def _head_rows_order():
    lane_tiles = XA_HEAD_DIM // LANES
    return [(r % XA_HEADS) * lane_tiles + r // XA_HEADS for r in range(XA_HEADS * lane_tiles)]


def _to_head_slabs(a):
    lead = a.shape[:-2]
    lane_tiles = XA_HEAD_DIM // LANES
    nd = len(lead)
    a = a.reshape(lead + (XA_HEADS, lane_tiles, LANES))
    a = a.transpose(tuple(range(nd)) + (nd + 1, nd, nd + 2))
    return a.reshape(lead + (XA_HEADS * lane_tiles, LANES))


def _from_head_slabs(a):
    lead = a.shape[:-2]
    lane_tiles = XA_HEAD_DIM // LANES
    nd = len(lead)
    a = a.reshape(lead + (lane_tiles, XA_HEADS, LANES))
    a = a.transpose(tuple(range(nd)) + (nd + 1, nd, nd + 2))
    return a.reshape(lead + (XA_HEADS, XA_HEAD_DIM))


def _kv_proj_kernel(mem_ref, wk_ref, wv_ref, k_ref, v_ref, k8_ref, v8_ref):
    m = mem_ref[...].astype(BF16)
    k = _dot(m, wk_ref[0].astype(BF16))
    v = _dot(m, wv_ref[0].astype(BF16))
    k_ref[0] = k
    v_ref[0] = v
    for r, j in enumerate(_head_rows_order()):
        k8_ref[0, :, r, :] = k[:, j * LANES:(j + 1) * LANES]
        v8_ref[0, :, r, :] = v[:, j * LANES:(j + 1) * LANES]


def _kv_proj(mem2d, w_k, w_v):
    rows = mem2d.shape[0]
    depth = w_k.shape[0]
    tr = min(rows, 512)
    out = jax.ShapeDtypeStruct((depth, rows, D_MODEL), F32)
    out8 = jax.ShapeDtypeStruct((depth, rows, D_MODEL // LANES, LANES), F32)
    return pl.pallas_call(
        _kv_proj_kernel,
        grid=(depth, rows // tr),
        in_specs=[pl.BlockSpec((tr, D_MODEL), lambda l, r: (r, 0)),
                  pl.BlockSpec((1, D_MODEL, D_MODEL), lambda l, r: (l, 0, 0)),
                  pl.BlockSpec((1, D_MODEL, D_MODEL), lambda l, r: (l, 0, 0))],
        out_specs=[pl.BlockSpec((1, tr, D_MODEL), lambda l, r: (l, r, 0)),
                   pl.BlockSpec((1, tr, D_MODEL), lambda l, r: (l, r, 0)),
                   pl.BlockSpec((1, tr, D_MODEL // LANES, LANES), lambda l, r: (l, r, 0, 0)),
                   pl.BlockSpec((1, tr, D_MODEL // LANES, LANES), lambda l, r: (l, r, 0, 0))],
        out_shape=[out, out, out8, out8],
        compiler_params=_params(("arbitrary", "arbitrary")),
        name="kv_proj",
    )(mem2d, w_k, w_v)


def _even_prompt_kernel(x_ref, g_ref, w_in_ref, pw_ref, ps_ref, cw_ref, w_out_ref,
                        xo_ref, pool_ref, conv_ref, ext_ref, cext_ref, *, tile):
    t = pl.program_id(1)
    D = D_MODEL

    @pl.when(t == 0)
    def _():
        ext_ref[0:HALO, :] = jnp.zeros((HALO, D), F32)
        cext_ref[0:HALO, :] = jnp.zeros((HALO, D), F32)

    @pl.when(t > 0)
    def _():
        ext_ref[0:HALO, :] = ext_ref[tile:tile + HALO, :]
        cext_ref[0:HALO, :] = cext_ref[tile:tile + HALO, :]

    x = x_ref[0]
    h = _rmsnorm(x, g_ref[...]).astype(BF16)

    def proj(j):
        return _dot(h, w_in_ref[:, j * D:(j + 1) * D])

    u = proj(0)
    ext_ref[HALO:HALO + tile, :] = u
    g_pool = proj(1)
    pos = t * tile + lax.broadcasted_iota(jnp.int32, (tile, POOL_GROUP), 0)
    out = jnp.zeros((tile, D), F32)
    for g, w in enumerate(POOL_WINDOWS):
        sl = slice(g * POOL_GROUP, (g + 1) * POOL_GROUP)
        s = ext_ref[:, sl]
        k = 1
        while k < w:
            s = s + pltpu.roll(s, k, 0)
            k *= 2
        cnt = jnp.minimum(pos + 1, w).astype(F32)
        diff = s[HALO:, :] / cnt - u[:, sl]
        yp = _dot(diff.astype(BF16), pw_ref[g]) * ps_ref[:, sl] * _silu(g_pool[:, sl])
        out = out + _dot(yp.astype(BF16), w_out_ref[sl, :])

    b_gate = proj(2)
    cv = proj(3) * proj(4)
    cext_ref[HALO:HALO + tile, :] = cv
    ce = cext_ref[...]
    conv = cw_ref[2:3, :] * cv
    conv = conv + cw_ref[1:2, :] * pltpu.roll(ce, 1, 0)[HALO:, :]
    conv = conv + cw_ref[0:1, :] * pltpu.roll(ce, 2, 0)[HALO:, :]
    yc = b_gate * conv * _silu(proj(5))
    out = out + _dot(yc.astype(BF16), w_out_ref[D:2 * D, :])
    xo_ref[0] = x + out

    @pl.when(t == pl.num_programs(1) - 1)
    def _():
        pool_ref[0] = ext_ref[tile:tile + HALO, :]
        conv_ref[0] = cext_ref[tile:tile + HALO, :]


def _even_prompt(x, consts, tile):
    B, L, D = x.shape
    kern = functools.partial(_even_prompt_kernel, tile=tile)
    arrays, specs = _operands(consts)
    return pl.pallas_call(
        kern,
        grid=(B, L // tile),
        in_specs=[pl.BlockSpec((1, tile, D), lambda b, t: (b, t, 0))] + specs,
        out_specs=[pl.BlockSpec((1, tile, D), lambda b, t: (b, t, 0)),
                   pl.BlockSpec((1, HALO, D), lambda b, t: (b, 0, 0)),
                   pl.BlockSpec((1, HALO, D), lambda b, t: (b, 0, 0))],
        out_shape=[jax.ShapeDtypeStruct((B, L, D), F32),
                   jax.ShapeDtypeStruct((B, HALO, D), F32),
                   jax.ShapeDtypeStruct((B, HALO, D), F32)],
        scratch_shapes=[pltpu.VMEM((HALO + tile, D), F32), pltpu.VMEM((HALO + tile, D), F32)],
        compiler_params=_params(("arbitrary", "arbitrary")),
        name="even_prompt",
    )(x, *arrays)


def _xattn_prompt_kernel(x_ref, g_ref, wq_ref, wo_ref, k_ref, v_ref, gf_ref, xo_ref, *, final):
    x = x_ref[0]
    h = _rmsnorm(x, g_ref[...]).astype(BF16)
    q = (_dot(h, wq_ref[...]) * (XA_HEAD_DIM ** -0.5)).astype(BF16)
    out = jnp.zeros_like(x)
    for hd in range(XA_HEADS):
        sl = slice(hd * XA_HEAD_DIM, (hd + 1) * XA_HEAD_DIM)
        kh = k_ref[0, :, sl].astype(BF16)
        vh = v_ref[0, :, sl].astype(BF16)
        s = _dot_nt(q[:, sl], kh)
        e = jnp.exp(s - jnp.max(s, axis=-1, keepdims=True))
        o = _dot(e.astype(BF16), vh) / jnp.sum(e, axis=-1, keepdims=True)
        out = out + _dot(o.astype(BF16), wo_ref[sl, :])
    y = x + out
    if final:
        y = _rmsnorm(y, gf_ref[...])
    xo_ref[0] = y


def _xattn_prompt(x, consts, k, v, layer, gf, tile, final):
    B, L, D = x.shape
    kern = functools.partial(_xattn_prompt_kernel, final=final)
    arrays, specs = _operands(consts)
    return pl.pallas_call(
        kern,
        grid=(B, L // tile),
        in_specs=[pl.BlockSpec((1, tile, D), lambda b, t: (b, t, 0))] + specs + [
            pl.BlockSpec((None, 1, N_MEM, D), lambda b, t: (layer, b, 0, 0)),
            pl.BlockSpec((None, 1, N_MEM, D), lambda b, t: (layer, b, 0, 0)),
            _const_spec(gf.shape)],
        out_specs=pl.BlockSpec((1, tile, D), lambda b, t: (b, t, 0)),
        out_shape=jax.ShapeDtypeStruct((B, L, D), F32),
        compiler_params=_params(("arbitrary", "arbitrary")),
        name="xattn_prompt",
    )(x, *arrays, k, v, gf)


def _odd_prompt_kernel(x_ref, g_ref, wz_ref, wxbc_ref, wdt_ref, cw_ref, cb_ref, dtb_ref, alog_ref,
                       dskip_ref, nw_ref, w_out_ref,
                       xo_ref, sconv_ref, state_ref, xext_ref, st_ref, y_ref, *, tile):
    t = pl.program_id(1)
    Q = SSM_CHUNK
    C = SSM_CONV_DIM

    @pl.when(t == 0)
    def _():
        xext_ref[0:HALO, :] = jnp.zeros((HALO, C), F32)
        st_ref[...] = jnp.zeros(st_ref.shape, F32)

    @pl.when(t > 0)
    def _():
        xext_ref[0:HALO, :] = xext_ref[tile:tile + HALO, :]

    x = x_ref[0]
    h = _rmsnorm(x, g_ref[...]).astype(BF16)
    xbc = _dot(h, wxbc_ref[...])
    xext_ref[HALO:HALO + tile, :] = xbc
    xe = xext_ref[...]
    conv = cw_ref[3:4, :] * xbc
    for k in range(1, SSM_CONV_K):
        conv = conv + cw_ref[3 - k:4 - k, :] * pltpu.roll(xe, k, 0)[HALO:, :]
    xc = _silu(conv + cb_ref[...])
    dt = _softplus(_dot(h, wdt_ref[...]) + dtb_ref[...])
    dta = dt * (-jnp.exp(alog_ref[...]) * LOG2E)

    row = lax.broadcasted_iota(jnp.int32, (Q, Q), 0)
    col = lax.broadcasted_iota(jnp.int32, (Q, Q), 1)
    causal = row >= col
    tri = jnp.where(causal, 1.0, 0.0).astype(BF16)
    lo_lane = lax.broadcasted_iota(jnp.int32, (1, LANES), 1) < SSM_HEAD_DIM
    neg_inf = jnp.float32(-jnp.inf)

    for c in range(tile // Q):
        rs = slice(c * Q, (c + 1) * Q)
        a_cs = _dot_exact_rhs(tri, dta[rs, :])
        a_cs_t = a_cs.T
        dt_t = dt[rs, :].T
        w_t = dt_t * jnp.exp2(a_cs_t[:, Q - 1:Q] - a_cs_t)
        ea = jnp.exp2(a_cs)
        dec = ea[Q - 1:Q, :]
        for g in range(SSM_GROUPS):
            bsl = slice(D_INNER + g * SSM_STATE, D_INNER + (g + 1) * SSM_STATE)
            csl = slice(D_INNER + SSM_GROUPS * SSM_STATE + g * SSM_STATE,
                        D_INNER + SSM_GROUPS * SSM_STATE + (g + 1) * SSM_STATE)
            b_t = xc[rs, bsl].T
            c_g = xc[rs, csl].astype(BF16)
            cb = _dot(c_g, b_t.astype(BF16))
            gsl = slice(g * SSM_GROUP_W, (g + 1) * SSM_GROUP_W)
            y_off = _dot(c_g, st_ref[:, gsl].astype(BF16))
            for j in range(SSM_GROUP_W // LANES):
                h0 = (g * SSM_GROUP_W + j * LANES) // SSM_HEAD_DIM
                lsl = slice(g * SSM_GROUP_W + j * LANES, g * SSM_GROUP_W + (j + 1) * LANES)
                xs = xc[rs, lsl]
                wmat = jnp.concatenate([jnp.where(lo_lane, xs, 0.0), jnp.where(lo_lane, 0.0, xs)],
                                       axis=0).astype(BF16)
                tops, bots = [], []
                for hh in (h0, h0 + 1):
                    seg = a_cs[:, hh:hh + 1] - a_cs_t[hh:hh + 1, :]
                    decay = jnp.exp2(jnp.where(causal, seg, neg_inf))
                    tops.append(cb * decay * dt_t[hh:hh + 1, :])
                    bots.append(b_t * w_t[hh:hh + 1, :])
                lhs = jnp.concatenate([jnp.concatenate(tops, axis=1), jnp.concatenate(bots, axis=1)],
                                      axis=0).astype(BF16)
                r = _dot(lhs, wmat)
                escale = jnp.where(lo_lane, ea[:, h0:h0 + 1], ea[:, h0 + 1:h0 + 2])
                y_ref[rs, lsl] = (r[0:Q, :] + y_off[:, j * LANES:(j + 1) * LANES] * escale
                                  + dskip_ref[:, lsl] * xs)
                dscale = jnp.where(lo_lane, dec[:, h0:h0 + 1], dec[:, h0 + 1:h0 + 2])
                st_ref[:, lsl] = st_ref[:, lsl] * dscale + r[Q:2 * Q, :]

    yg = y_ref[...] * _silu(_dot(h, wz_ref[...]))
    out = jnp.zeros((tile, D_MODEL), F32)
    for g in range(SSM_GROUPS):
        gsl = slice(g * SSM_GROUP_W, (g + 1) * SSM_GROUP_W)
        ygg = yg[:, gsl]
        yn = ygg * lax.rsqrt(jnp.mean(ygg * ygg, axis=-1, keepdims=True) + EPS) * nw_ref[:, gsl]
        out = out + _dot(yn.astype(BF16), w_out_ref[gsl, :])
    xo_ref[0] = x + out

    @pl.when(t == pl.num_programs(1) - 1)
    def _():
        sconv_ref[0] = xext_ref[tile:tile + HALO, :]
        state_ref[0] = st_ref[...].T


def _odd_prompt(x, consts, tile):
    B, L, D = x.shape
    kern = functools.partial(_odd_prompt_kernel, tile=tile)
    arrays, specs = _operands(consts)
    return pl.pallas_call(
        kern,
        grid=(B, L // tile),
        in_specs=[pl.BlockSpec((1, tile, D), lambda b, t: (b, t, 0))] + specs,
        out_specs=[pl.BlockSpec((1, tile, D), lambda b, t: (b, t, 0)),
                   pl.BlockSpec((1, HALO, SSM_CONV_DIM), lambda b, t: (b, 0, 0)),
                   pl.BlockSpec((1, D_INNER, SSM_STATE), lambda b, t: (b, 0, 0))],
        out_shape=[jax.ShapeDtypeStruct((B, L, D), F32),
                   jax.ShapeDtypeStruct((B, HALO, SSM_CONV_DIM), F32),
                   jax.ShapeDtypeStruct((B, D_INNER, SSM_STATE), F32)],
        scratch_shapes=[pltpu.VMEM((HALO + tile, SSM_CONV_DIM), F32),
                        pltpu.VMEM((SSM_STATE, D_INNER), F32),
                        pltpu.VMEM((tile, D_INNER), F32)],
        compiler_params=_params(("arbitrary", "arbitrary")),
        name="odd_prompt",
    )(x, *arrays)


def _finish_xattn(x, o8_ref, wo_ref):
    for r, j in enumerate(_head_rows_order()):
        x = x + _dot(o8_ref[:, r, :].astype(BF16), wo_ref[j * LANES:(j + 1) * LANES, :])
    return x


def _emit_query(x, g_ref, wq_ref, q8_ref):
    q = _dot(_rmsnorm(x, g_ref[...]).astype(BF16), wq_ref[...]) * (XA_HEAD_DIM ** -0.5 * LOG2E)
    for r, j in enumerate(_head_rows_order()):
        q8_ref[:, r, :] = q[:, j * LANES:(j + 1) * LANES]


def _even_sample_kernel(*refs, has_prev):
    refs = list(refs)
    x_ref = refs.pop(0)
    x = x_ref[...]
    if has_prev:
        o8_ref, wo_ref = refs.pop(0), refs.pop(0)
        x = _finish_xattn(x, o8_ref, wo_ref)
    (g_ref, w_in_ref, pw_ref, ps_ref, cw_ref, w_out_ref, pool_ref, cst_ref, gq_ref, wq_ref,
     xo_ref, q8_ref, pool_o_ref, cst_o_ref) = refs
    D = D_MODEL
    h = _rmsnorm(x, g_ref[...]).astype(BF16)

    def proj(j):
        return _dot(h, w_in_ref[:, j * D:(j + 1) * D])

    u = proj(0)
    g_pool = proj(1)
    out = jnp.zeros_like(x)
    for g, w in enumerate(POOL_WINDOWS):
        sl = slice(g * POOL_GROUP, (g + 1) * POOL_GROUP)
        s = u[:, sl]
        for j in range(POOL_BUF - (w - 1), POOL_BUF):
            s = s + pool_ref[j, :, sl]
        cnt = float(min(PAST_LEN + 1, w))
        diff = s / cnt - u[:, sl]
        yp = _dot(diff.astype(BF16), pw_ref[g]) * ps_ref[:, sl] * _silu(g_pool[:, sl])
        out = out + _dot(yp.astype(BF16), w_out_ref[sl, :])
    for j in range(POOL_BUF - 1):
        pool_o_ref[j] = pool_ref[j + 1]
    pool_o_ref[POOL_BUF - 1] = u

    b_gate = proj(2)
    cv = proj(3) * proj(4)
    conv = cw_ref[0:1, :] * cst_ref[:, 0, :] + cw_ref[1:2, :] * cst_ref[:, 1, :] + cw_ref[2:3, :] * cv
    cst_o_ref[:, 0, :] = cst_ref[:, 1, :]
    cst_o_ref[:, 1, :] = cv
    yc = b_gate * conv * _silu(proj(5))
    out = out + _dot(yc.astype(BF16), w_out_ref[D:2 * D, :])
    x = x + out
    xo_ref[...] = x
    _emit_query(x, gq_ref, wq_ref, q8_ref)


class _StackedOut:
    def __init__(self, shape, layer, prev):
        self.shape, self.layer, self.prev = shape, layer, prev


def _sample_call(kern, ops, outs, name):
    args, specs = _operands(ops)
    n_in = len(args)
    out_specs, out_shapes, aliases = [], [], {}
    for k, o in enumerate(outs):
        if isinstance(o, _StackedOut):
            nd = len(o.shape) - 1
            out_specs.append(pl.BlockSpec((None,) + o.shape[1:], lambda *_, l=o.layer, nd=nd: (l,) + (0,) * nd))
            out_shapes.append(jax.ShapeDtypeStruct(o.shape, F32))
            if o.prev is not None:
                aliases[len(args)] = k
                args.append(o.prev)
                specs.append(pl.BlockSpec(memory_space=pl.ANY))
        else:
            out_specs.append(_const_spec(o))
            out_shapes.append(jax.ShapeDtypeStruct(o, F32))
    n_alias = len(args) - n_in

    def body(*refs):
        kern(*refs[:n_in], *refs[n_in + n_alias:])

    return pl.pallas_call(
        body,
        grid=(1,),
        in_specs=specs,
        out_specs=out_specs,
        out_shape=out_shapes,
        input_output_aliases=aliases,
        compiler_params=_params(("arbitrary",)),
        name=name,
    )(*args)


def _slab_shape(rows):
    return (rows, D_MODEL // LANES, LANES)


def _even_sample(x, prev, consts, pool, pool_out, cst, q_consts):
    ops = [x] + (list(prev) if prev else []) + list(consts) + [pool, cst] + list(q_consts)
    kern = functools.partial(_even_sample_kernel, has_prev=prev is not None)
    outs = [x.shape, _slab_shape(x.shape[0]), _StackedOut(pool[0].shape, pool[1], pool_out), cst[0].shape[1:]]
    return _sample_call(kern, ops, outs, "even_sample")


def _xattn_sample_stream_kernel(q_ref, k_ref, v_ref, o_ref, *, bt):
    slab_rows, lanes = q_ref.shape[1:]
    ones = jnp.ones((lanes, lanes), BF16)
    for b in range(bt):
        prod = (k_ref[b] * q_ref[b][None]).reshape(N_MEM * slab_rows, lanes)
        r = _dot(prod.astype(BF16), ones).reshape(N_MEM, slab_rows, lanes)
        s = r + pltpu.roll(r, XA_HEADS, 1)
        e = jnp.exp2(s - jnp.max(s, axis=0, keepdims=True))
        o_ref[b] = jnp.sum(e * v_ref[b], axis=0) / jnp.sum(e, axis=0)


def _xattn_sample_stream(q8, k8, v8, layer, bt):
    DB = q8.shape[0]
    slab = q8.shape[1:]
    kern = functools.partial(_xattn_sample_stream_kernel, bt=bt)
    return pl.pallas_call(
        kern,
        grid=(DB // bt,),
        in_specs=[pl.BlockSpec((bt,) + slab, lambda i: (i, 0, 0)),
                  pl.BlockSpec((None, bt, N_MEM) + slab, lambda i: (layer, i, 0, 0, 0)),
                  pl.BlockSpec((None, bt, N_MEM) + slab, lambda i: (layer, i, 0, 0, 0))],
        out_specs=pl.BlockSpec((bt,) + slab, lambda i: (i, 0, 0)),
        out_shape=jax.ShapeDtypeStruct(q8.shape, F32),
        compiler_params=_params(("arbitrary",)),
        name="xattn_sample_stream",
    )(q8, k8, v8)


def _xattn_sample_final_kernel(x_ref, o8_ref, wo_ref, gf_ref, xo_ref):
    xo_ref[...] = _rmsnorm(_finish_xattn(x_ref[...], o8_ref, wo_ref), gf_ref[...])


def _xattn_sample_final(x, o8, wo, gf):
    return _sample_call(_xattn_sample_final_kernel, (x, o8, wo, gf), [x.shape], "xattn_sample_final")[0]


def _odd_sample_in_kernel(x_ref, o8_ref, wo_ref, g_ref, wz_ref, wxbc_ref, wdt_ref, cw_ref, cb_ref, dtb_ref, alog_ref,
                          sconv_ref, x1_ref, z_ref, xs_ref, bm_ref, cm_ref, xdt_ref, dec_ref, sconv_o_ref):
    x = _finish_xattn(x_ref[...], o8_ref, wo_ref)
    x1_ref[...] = x
    h = _rmsnorm(x, g_ref[...]).astype(BF16)
    xbc = _dot(h, wxbc_ref[...])
    conv = cw_ref[3:4, :] * xbc
    for k in range(SSM_CONV_K - 1):
        conv = conv + cw_ref[k:k + 1, :] * sconv_ref[k]
    for k in range(SSM_CONV_K - 2):
        sconv_o_ref[k] = sconv_ref[k + 1]
    sconv_o_ref[SSM_CONV_K - 2] = xbc
    xc = _silu(conv + cb_ref[...])
    xs = xc[:, :D_INNER]
    dt = _softplus(_dot(h, wdt_ref[...]) + dtb_ref[...])
    dta = dt * (-jnp.exp(alog_ref[...]))
    hrow = lax.broadcasted_iota(jnp.int32, (LANES, D_INNER), 0)
    ccol = lax.broadcasted_iota(jnp.int32, (LANES, D_INNER), 1)
    expand = jnp.where(ccol // SSM_HEAD_DIM == hrow, 1.0, 0.0).astype(BF16)
    z_ref[...] = _dot(h, wz_ref[...])
    xs_ref[...] = xs
    bm_ref[...] = xc[:, D_INNER:D_INNER + SSM_GROUPS * SSM_STATE]
    cm_ref[...] = xc[:, D_INNER + SSM_GROUPS * SSM_STATE:]
    xdt_ref[...] = xs * _dot_exact_lhs(dt, expand)
    dec_ref[...] = jnp.exp(dta)


def _odd_sample_in(x, prev, consts, sconv, sconv_out):
    DB = x.shape[0]
    ops = [x] + list(prev) + list(consts) + [sconv]
    outs = [x.shape, (DB, D_INNER), (DB, D_INNER), (DB, SSM_GROUPS * SSM_STATE), (DB, SSM_GROUPS * SSM_STATE),
            (DB, D_INNER), (DB, LANES), _StackedOut(sconv[0].shape, sconv[1], sconv_out)]
    return _sample_call(_odd_sample_in_kernel, ops, outs, "odd_sample_in")


def _odd_sample_state_kernel(dec_ref, xdt_ref, bm_ref, cm_ref, s_ref, *rest, bt):
    so_ref, y_ref = rest[-2:]
    i = pl.program_id(0)
    pad = jnp.zeros((LANES - bt, D_INNER), F32)
    xdt_t = jnp.concatenate([xdt_ref[...], pad], axis=0).T.astype(BF16)
    padn = jnp.zeros((LANES - bt, SSM_STATE), F32)
    row = lax.broadcasted_iota(jnp.int32, (LANES, SSM_STATE), 0)
    y_acc = [jnp.zeros((SSM_GROUP_W, LANES), F32) for _ in range(SSM_GROUPS)]
    heads_per_group = SSM_GROUP_W // SSM_HEAD_DIM
    for g in range(SSM_GROUPS):
        nsl = slice(g * SSM_STATE, (g + 1) * SSM_STATE)
        gsl = slice(g * SSM_GROUP_W, (g + 1) * SSM_GROUP_W)
        b_all = jnp.concatenate([bm_ref[:, nsl], padn], axis=0)
        c_all = jnp.concatenate([cm_ref[:, nsl], padn], axis=0)
        for b in range(bt):
            b_sel = jnp.where(row == b, b_all, 0.0).astype(BF16)
            c_sel = jnp.where(row == b, c_all, 0.0).astype(BF16)
            upd = _dot(xdt_t[gsl, :], b_sel)
            pieces = []
            for e in range(heads_per_group):
                hd = g * heads_per_group + e
                rsl = slice(hd * SSM_HEAD_DIM, (hd + 1) * SSM_HEAD_DIM)
                new = s_ref[b, rsl, :] * dec_ref[i * bt + b, hd] + upd[e * SSM_HEAD_DIM:(e + 1) * SSM_HEAD_DIM, :]
                so_ref[b, rsl, :] = new
                pieces.append(new)
            y_acc[g] = y_acc[g] + _dot_nt(jnp.concatenate(pieces, axis=0).astype(BF16), c_sel)
    y_ref[...] = jnp.concatenate(y_acc, axis=0).T[0:bt, :]


def _odd_sample_state(dec, xdt, bm, cm, state_all, layer, prev_out, bt):
    DB = xdt.shape[0]
    kern = functools.partial(_odd_sample_state_kernel, bt=bt)
    gn = SSM_GROUPS * SSM_STATE
    in_specs = [pl.BlockSpec(memory_space=pltpu.SMEM),
                pl.BlockSpec((bt, D_INNER), lambda i: (i, 0)),
                pl.BlockSpec((bt, gn), lambda i: (i, 0)),
                pl.BlockSpec((bt, gn), lambda i: (i, 0)),
                pl.BlockSpec((None, bt, D_INNER, SSM_STATE), lambda i: (layer, i, 0, 0))]
    args = [dec, xdt, bm, cm, state_all]
    aliases = {}
    if prev_out is not None:
        in_specs.append(pl.BlockSpec(memory_space=pl.ANY))
        args.append(prev_out)
        aliases = {len(args) - 1: 0}
    return pl.pallas_call(
        kern,
        grid=(DB // bt,),
        in_specs=in_specs,
        out_specs=[pl.BlockSpec((None, bt, D_INNER, SSM_STATE), lambda i: (layer, i, 0, 0)),
                   pl.BlockSpec((bt, D_INNER), lambda i: (i, 0))],
        out_shape=[jax.ShapeDtypeStruct(state_all.shape, F32), jax.ShapeDtypeStruct((DB, D_INNER), F32)],
        input_output_aliases=aliases,
        compiler_params=_params(("arbitrary",)),
        name="odd_sample_state",
    )(*args)


def _odd_sample_out_kernel(x_ref, y_ref, xs_ref, z_ref, dskip_ref, nw_ref, w_out_ref, gq_ref, wq_ref,
                           xo_ref, q8_ref):
    yg = (y_ref[...] + dskip_ref[...] * xs_ref[...]) * _silu(z_ref[...])
    out = jnp.zeros(x_ref.shape, F32)
    for g in range(SSM_GROUPS):
        gsl = slice(g * SSM_GROUP_W, (g + 1) * SSM_GROUP_W)
        ygg = yg[:, gsl]
        yn = ygg * lax.rsqrt(jnp.mean(ygg * ygg, axis=-1, keepdims=True) + EPS) * nw_ref[:, gsl]
        out = out + _dot(yn.astype(BF16), w_out_ref[gsl, :])
    x = x_ref[...] + out
    xo_ref[...] = x
    _emit_query(x, gq_ref, wq_ref, q8_ref)


def _odd_sample_out(x, y, xs, z, consts, q_consts):
    ops = [x, y, xs, z] + list(consts) + list(q_consts)
    return _sample_call(_odd_sample_out_kernel, ops, [x.shape, _slab_shape(x.shape[0])], "odd_sample_out")


def _pad_lanes(v):
    return jnp.pad(v, ((0, 0), (0, LANES - v.shape[-1])))


def kernel(x_prompt, x_sample, mem_prompt, cache_mem_k, cache_mem_v, state_pool, state_conv, state_ssm_conv,
           state_ssm, norm_mix, norm_xa, norm_final, w_in_even, pool_w, pool_scale, conv_w, w_out_even, w_in_odd,
           ssm_conv_w, ssm_conv_b, dt_bias, a_log, d_skip, ssm_norm, w_out_odd, w_xa_q, w_xa_k, w_xa_v, w_xa_o):
    B, L, D = x_prompt.shape
    DB = x_sample.shape[0]
    depth = norm_mix.shape[0]
    tile = min(L, 512)
    tile_odd = min(L, 512)
    tile_xa = min(L, 1024)
    bt_x = min(DB, 8)
    bt_s = min(DB, 8)

    mem_k, mem_v, mem_k8, mem_v8 = _kv_proj(mem_prompt.reshape(B * N_MEM, D), w_xa_k, w_xa_v)
    mem_k = mem_k.reshape(depth, B, N_MEM, D)
    mem_v = mem_v.reshape(depth, B, N_MEM, D)
    slab = (D // LANES, LANES)
    cache_k8 = _to_head_slabs(cache_mem_k)
    cache_v8 = _to_head_slabs(cache_mem_v)
    ssm_all = state_ssm.reshape(state_ssm.shape[0], DB, D_INNER, SSM_STATE)
    ssm_s = None

    def rows(a):
        return a.reshape(a.shape[0], 1, a.shape[-1])

    g_mix_all, g_xa_all = rows(norm_mix), rows(norm_xa)
    w_in_e, pw_e, ps_e = w_in_even.astype(BF16), pool_w.astype(BF16), rows(pool_scale)
    w_out_e = w_out_even.astype(BF16)
    w_in_o = w_in_odd.astype(BF16)
    wz_o = w_in_o[:, :, :D_INNER]
    wxbc_o = w_in_o[:, :, D_INNER:D_INNER + SSM_CONV_DIM]
    wdt_o = jnp.pad(w_in_o[:, :, D_INNER + SSM_CONV_DIM:], ((0, 0), (0, 0), (0, LANES - SSM_HEADS)))
    cb_o, dtb_o, alog_o = rows(ssm_conv_b), rows(_pad_lanes(dt_bias)), rows(_pad_lanes(a_log))
    dskip_o, nw_o = rows(jnp.repeat(d_skip, SSM_HEAD_DIM, axis=1)), rows(ssm_norm)
    w_out_o = w_out_odd.astype(BF16)
    wq_all, wo_all = w_xa_q.astype(BF16), w_xa_o.astype(BF16)

    xp = x_prompt
    xs_ = x_sample.reshape(DB, D)
    gf = norm_final.reshape(1, D)
    pool_p, conv_p, conv_s, sconv_p, ssm_p = [], [], [], [], []
    pool_t, sconv_t = state_pool.transpose(0, 2, 1, 3), state_ssm_conv.transpose(0, 2, 1, 3)
    pool_s = sconv_s = None
    prev = None
    for l in range(depth):
        i = l // 2
        g_mix = _layer(g_mix_all, l)
        g_xa, wq, wo = _layer(g_xa_all, l), _layer(wq_all, l), _layer(wo_all, l)
        if l % 2 == 0:
            consts = [g_mix, _layer(w_in_e, i), _layer(pw_e, i), _layer(ps_e, i), _layer(conv_w, i),
                      _layer(w_out_e, i)]
            xp, pb, cbuf = _even_prompt(xp, consts, tile)
            pool_p.append(pb[:, HALO - POOL_BUF:, :])
            conv_p.append(cbuf[:, HALO - (CONV_K - 1):, :])
            xs_, q8, pool_s, cst_n = _even_sample(xs_, prev, consts, _layer(pool_t, i), pool_s,
                                                  _layer(state_conv, i), (g_xa, wq))
            conv_s.append(cst_n)
        else:
            in_consts = [g_mix, _layer(wz_o, i), _layer(wxbc_o, i), _layer(wdt_o, i), _layer(ssm_conv_w, i),
                         _layer(cb_o, i), _layer(dtb_o, i), _layer(alog_o, i)]
            out_consts = [_layer(dskip_o, i), _layer(nw_o, i), _layer(w_out_o, i)]
            xp, sb, st = _odd_prompt(xp, in_consts + out_consts, tile_odd)
            sconv_p.append(sb[:, HALO - (SSM_CONV_K - 1):, :])
            ssm_p.append(st.reshape(B, SSM_HEADS, SSM_HEAD_DIM, SSM_STATE))
            xs_, z, xs_in, bm, cm, xdt, dec, sconv_s = _odd_sample_in(xs_, prev, in_consts,
                                                                      _layer(sconv_t, i), sconv_s)
            ssm_s, y = _odd_sample_state(dec, xdt, bm, cm, ssm_all, i, ssm_s, bt_s)
            xs_, q8 = _odd_sample_out(xs_, y, xs_in, z, out_consts, (g_xa, wq))
        final = l == depth - 1
        xp = _xattn_prompt(xp, [g_xa, wq, wo], mem_k, mem_v, l, gf, tile_xa, final)
        prev = (_xattn_sample_stream(q8, cache_k8, cache_v8, l, bt_x), wo)
    xs_ = _xattn_sample_final(xs_, prev[0], prev[1], gf)

    mem_shape = (depth, B, N_MEM) + slab
    return (xp, xs_.reshape(DB, 1, D),
            _from_head_slabs(mem_k8.reshape(mem_shape)), _from_head_slabs(mem_v8.reshape(mem_shape)),
            jnp.stack(pool_p), pool_s.transpose(0, 2, 1, 3), jnp.stack(conv_p), jnp.stack(conv_s),
            jnp.stack(sconv_p), sconv_s.transpose(0, 2, 1, 3), jnp.stack(ssm_p),
            ssm_s.reshape(ssm_all.shape[0], DB, SSM_HEADS, SSM_HEAD_DIM, SSM_STATE))
```

```python
import functools

import jax
import jax.numpy as jnp
from jax import lax
from jax.experimental import pallas as pl
from jax.experimental.pallas import tpu as pltpu

F32 = jnp.float32
BF16 = jnp.bfloat16

D_MODEL = 1024
PAST_LEN = 16384
POOL_WINDOWS = (2, 4, 8, 16)
POOL_GROUP = D_MODEL // len(POOL_WINDOWS)
POOL_BUF = max(POOL_WINDOWS) - 1
CONV_K = 3
D_INNER = 2 * D_MODEL
SSM_HEAD_DIM = 64
SSM_HEADS = D_INNER // SSM_HEAD_DIM
SSM_GROUPS = 4
SSM_STATE = 128
SSM_GROUP_W = D_INNER // SSM_GROUPS
SSM_CONV_K = 4
SSM_CONV_DIM = D_INNER + 2 * SSM_GROUPS * SSM_STATE
SSM_CHUNK = 128
N_MEM = 256
XA_HEADS = 4
XA_HEAD_DIM = D_MODEL // XA_HEADS
EPS = 1e-6
LOG2E = 1.4426950408889634
LANES = 128
HALO = 16
VMEM_LIMIT = 56 * 1024 * 1024


def _rmsnorm(x, g):
    return x * lax.rsqrt(jnp.mean(x * x, axis=-1, keepdims=True) + EPS) * g


def _silu(x):
    return x * jax.nn.sigmoid(x)


def _softplus(x):
    return jnp.maximum(x, 0.0) + jnp.log1p(jnp.exp(-jnp.abs(x)))


def _dot(a, b):
    return jnp.dot(a, b, preferred_element_type=F32)


def _dot_nt(a, b):
    return lax.dot_general(a, b, (((1,), (1,)), ((), ())), preferred_element_type=F32)


def _dt_raw(h, wt_ref):
    raw = _dot_nt(h, wt_ref[D_INNER + SSM_CONV_DIM:, :])
    return jnp.concatenate([raw, jnp.zeros((raw.shape[0], LANES - SSM_HEADS), F32)], axis=1)


def _split3(x):
    hi = x.astype(BF16)
    r1 = x - hi.astype(F32)
    mid = r1.astype(BF16)
    lo = (r1 - mid.astype(F32)).astype(BF16)
    return hi, mid, lo


def _dot_exact_rhs(a_bf16, x):
    hi, mid, lo = _split3(x)
    return _dot(a_bf16, hi) + _dot(a_bf16, mid) + _dot(a_bf16, lo)


def _dot_exact_lhs(x, b_bf16):
    hi, mid, lo = _split3(x)
    return _dot(hi, b_bf16) + _dot(mid, b_bf16) + _dot(lo, b_bf16)


def _const_spec(shape):
    nd = len(shape)
    return pl.BlockSpec(shape, lambda *_: (0,) * nd, pipeline_mode=pl.Buffered(1))


def _layer(arr, l):
    return (arr, l)


def _operands(ops):
    arrays, specs = [], []
    for op in ops:
        if isinstance(op, tuple):
            arr, l = op
            nd = arr.ndim - 1
            specs.append(pl.BlockSpec((None,) + arr.shape[1:], lambda *_, l=l, nd=nd: (l,) + (0,) * nd,
                                      pipeline_mode=pl.Buffered(1)))
        else:
            arr = op
            specs.append(_const_spec(arr.shape))
        arrays.append(arr)
    return arrays, specs


def _params(sem):
    return pltpu.CompilerParams(dimension_semantics=sem, vmem_limit_bytes=VMEM_LIMIT)


def _head_rows_order():
    lane_tiles = XA_HEAD_DIM // LANES
    return [(r % XA_HEADS) * lane_tiles + r // XA_HEADS for r in range(XA_HEADS * lane_tiles)]


def _to_head_slabs(a):
    lead = a.shape[:-2]
    lane_tiles = XA_HEAD_DIM // LANES
    nd = len(lead)
    a = a.reshape(lead + (XA_HEADS, lane_tiles, LANES))
    a = a.transpose(tuple(range(nd)) + (nd + 1, nd, nd + 2))
    return a.reshape(lead + (XA_HEADS * lane_tiles, LANES))


def _from_head_slabs(a):
    lead = a.shape[:-2]
    lane_tiles = XA_HEAD_DIM // LANES
    nd = len(lead)
    a = a.reshape(lead + (lane_tiles, XA_HEADS, LANES))
    a = a.transpose(tuple(range(nd)) + (nd + 1, nd, nd + 2))
    return a.reshape(lead + (XA_HEADS, XA_HEAD_DIM))


def _kv_proj_kernel(mem_ref, wk_ref, wv_ref, k_ref, v_ref, k8_ref, v8_ref):
    m = mem_ref[...].astype(BF16)
    k = _dot(m, wk_ref[0].astype(BF16))
    v = _dot(m, wv_ref[0].astype(BF16))
    k_ref[0] = k
    v_ref[0] = v
    for r, j in enumerate(_head_rows_order()):
        k8_ref[0, :, r, :] = k[:, j * LANES:(j + 1) * LANES]
        v8_ref[0, :, r, :] = v[:, j * LANES:(j + 1) * LANES]


def _kv_proj(mem2d, w_k, w_v):
    rows = mem2d.shape[0]
    depth = w_k.shape[0]
    tr = min(rows, 512)
    out = jax.ShapeDtypeStruct((depth, rows, D_MODEL), F32)
    out8 = jax.ShapeDtypeStruct((depth, rows, D_MODEL // LANES, LANES), F32)
    return pl.pallas_call(
        _kv_proj_kernel,
        grid=(depth, rows // tr),
        in_specs=[pl.BlockSpec((tr, D_MODEL), lambda l, r: (r, 0)),
                  pl.BlockSpec((1, D_MODEL, D_MODEL), lambda l, r: (l, 0, 0)),
                  pl.BlockSpec((1, D_MODEL, D_MODEL), lambda l, r: (l, 0, 0))],
        out_specs=[pl.BlockSpec((1, tr, D_MODEL), lambda l, r: (l, r, 0)),
                   pl.BlockSpec((1, tr, D_MODEL), lambda l, r: (l, r, 0)),
                   pl.BlockSpec((1, tr, D_MODEL // LANES, LANES), lambda l, r: (l, r, 0, 0)),
                   pl.BlockSpec((1, tr, D_MODEL // LANES, LANES), lambda l, r: (l, r, 0, 0))],
        out_shape=[out, out, out8, out8],
        compiler_params=_params(("arbitrary", "arbitrary")),
        name="kv_proj",
    )(mem2d, w_k, w_v)


def _even_prompt_kernel(x_ref, g_ref, w_in_ref, pw_ref, ps_ref, cw_ref, w_out_ref,
                        xo_ref, pool_ref, conv_ref, ext_ref, cext_ref, *, tile):
    t = pl.program_id(1)
    D = D_MODEL

    @pl.when(t == 0)
    def _():
        ext_ref[0:HALO, :] = jnp.zeros((HALO, D), F32)
        cext_ref[0:HALO, :] = jnp.zeros((HALO, D), F32)

    @pl.when(t > 0)
    def _():
        ext_ref[0:HALO, :] = ext_ref[tile:tile + HALO, :]
        cext_ref[0:HALO, :] = cext_ref[tile:tile + HALO, :]

    x = x_ref[0]
    h = _rmsnorm(x, g_ref[...]).astype(BF16)

    def proj(j):
        return _dot(h, w_in_ref[:, j * D:(j + 1) * D])

    u = proj(0)
    ext_ref[HALO:HALO + tile, :] = u
    g_pool = proj(1)
    pos = t * tile + lax.broadcasted_iota(jnp.int32, (tile, POOL_GROUP), 0)
    out = jnp.zeros((tile, D), F32)
    for g, w in enumerate(POOL_WINDOWS):
        sl = slice(g * POOL_GROUP, (g + 1) * POOL_GROUP)
        s = ext_ref[:, sl]
        k = 1
        while k < w:
            s = s + pltpu.roll(s, k, 0)
            k *= 2
        cnt = jnp.minimum(pos + 1, w).astype(F32)
        diff = s[HALO:, :] / cnt - u[:, sl]
        yp = _dot(diff.astype(BF16), pw_ref[g]) * ps_ref[:, sl] * _silu(g_pool[:, sl])
        out = out + _dot(yp.astype(BF16), w_out_ref[sl, :])

    b_gate = proj(2)
    cv = proj(3) * proj(4)
    cext_ref[HALO:HALO + tile, :] = cv
    ce = cext_ref[...]
    conv = cw_ref[2:3, :] * cv
    conv = conv + cw_ref[1:2, :] * pltpu.roll(ce, 1, 0)[HALO:, :]
    conv = conv + cw_ref[0:1, :] * pltpu.roll(ce, 2, 0)[HALO:, :]
    yc = b_gate * conv * _silu(proj(5))
    out = out + _dot(yc.astype(BF16), w_out_ref[D:2 * D, :])
    xo_ref[0] = x + out

    @pl.when(t == pl.num_programs(1) - 1)
    def _():
        pool_ref[0] = ext_ref[tile:tile + HALO, :]
        conv_ref[0] = cext_ref[tile:tile + HALO, :]


def _even_prompt(x, consts, tile):
    B, L, D = x.shape
    kern = functools.partial(_even_prompt_kernel, tile=tile)
    arrays, specs = _operands(consts)
    return pl.pallas_call(
        kern,
        grid=(B, L // tile),
        in_specs=[pl.BlockSpec((1, tile, D), lambda b, t: (b, t, 0))] + specs,
        out_specs=[pl.BlockSpec((1, tile, D), lambda b, t: (b, t, 0)),
                   pl.BlockSpec((1, HALO, D), lambda b, t: (b, 0, 0)),
                   pl.BlockSpec((1, HALO, D), lambda b, t: (b, 0, 0))],
        out_shape=[jax.ShapeDtypeStruct((B, L, D), F32),
                   jax.ShapeDtypeStruct((B, HALO, D), F32),
                   jax.ShapeDtypeStruct((B, HALO, D), F32)],
        scratch_shapes=[pltpu.VMEM((HALO + tile, D), F32), pltpu.VMEM((HALO + tile, D), F32)],
        compiler_params=_params(("arbitrary", "arbitrary")),
        name="even_prompt",
    )(x, *arrays)


def _xattn_prompt_kernel(x_ref, g_ref, wq_ref, wo_ref, k_ref, v_ref, gf_ref, xo_ref, *, final):
    x = x_ref[0]
    h = _rmsnorm(x, g_ref[...]).astype(BF16)
    q = (_dot(h, wq_ref[...]) * (XA_HEAD_DIM ** -0.5 * LOG2E)).astype(BF16)
    out = jnp.zeros_like(x)
    for hd in range(XA_HEADS):
        sl = slice(hd * XA_HEAD_DIM, (hd + 1) * XA_HEAD_DIM)
        kh = k_ref[0, :, sl].astype(BF16)
        vh = v_ref[0, :, sl].astype(BF16)
        s = _dot_nt(q[:, sl], kh)
        e = jnp.exp2(s - jnp.max(s, axis=-1, keepdims=True))
        o = _dot(e.astype(BF16), vh) / jnp.sum(e, axis=-1, keepdims=True)
        out = out + _dot(o.astype(BF16), wo_ref[sl, :])
    y = x + out
    if final:
        y = _rmsnorm(y, gf_ref[...])
    xo_ref[0] = y


def _xattn_prompt(x, consts, k, v, layer, gf, tile, final):
    B, L, D = x.shape
    kern = functools.partial(_xattn_prompt_kernel, final=final)
    arrays, specs = _operands(consts)
    return pl.pallas_call(
        kern,
        grid=(B, L // tile),
        in_specs=[pl.BlockSpec((1, tile, D), lambda b, t: (b, t, 0))] + specs + [
            pl.BlockSpec((None, 1, N_MEM, D), lambda b, t: (layer, b, 0, 0)),
            pl.BlockSpec((None, 1, N_MEM, D), lambda b, t: (layer, b, 0, 0)),
            _const_spec(gf.shape)],
        out_specs=pl.BlockSpec((1, tile, D), lambda b, t: (b, t, 0)),
        out_shape=jax.ShapeDtypeStruct((B, L, D), F32),
        compiler_params=_params(("arbitrary", "arbitrary")),
        name="xattn_prompt",
    )(x, *arrays, k, v, gf)


def _odd_prompt_kernel(x_ref, g_ref, wt_ref, cw_ref, cb_ref, dtb_ref, alog_ref,
                       dskip_ref, nw_ref, w_out_ref,
                       xo_ref, sconv_ref, state_ref, xext_ref, st_ref, y_ref, *, tile):
    t = pl.program_id(1)
    Q = SSM_CHUNK
    C = SSM_CONV_DIM

    @pl.when(t == 0)
    def _():
        xext_ref[0:HALO, :] = jnp.zeros((HALO, C), F32)
        st_ref[...] = jnp.zeros(st_ref.shape, F32)

    @pl.when(t > 0)
    def _():
        xext_ref[0:HALO, :] = xext_ref[tile:tile + HALO, :]

    x = x_ref[0]
    h = _rmsnorm(x, g_ref[...]).astype(BF16)
    xbc = _dot_nt(h, wt_ref[D_INNER:D_INNER + SSM_CONV_DIM, :])
    xext_ref[HALO:HALO + tile, :] = xbc
    xe = xext_ref[...]
    conv = cw_ref[3:4, :] * xbc
    for k in range(1, SSM_CONV_K):
        conv = conv + cw_ref[3 - k:4 - k, :] * pltpu.roll(xe, k, 0)[HALO:, :]
    xc = _silu(conv + cb_ref[...])
    dt = _softplus(_dt_raw(h, wt_ref) + dtb_ref[...])
    dta = dt * (-jnp.exp(alog_ref[...]) * LOG2E)

    row = lax.broadcasted_iota(jnp.int32, (Q, Q), 0)
    col = lax.broadcasted_iota(jnp.int32, (Q, Q), 1)
    causal = row >= col
    tri = jnp.where(causal, 1.0, 0.0).astype(BF16)
    lo_lane = lax.broadcasted_iota(jnp.int32, (1, LANES), 1) < SSM_HEAD_DIM
    lo_tile = col < SSM_HEAD_DIM
    neg_inf = jnp.float32(-jnp.inf)

    for c in range(tile // Q):
        rs = slice(c * Q, (c + 1) * Q)
        a_cs = _dot_exact_rhs(tri, dta[rs, :])
        a_cs_t = a_cs.T
        dt_t = dt[rs, :].T
        w_t = dt_t * jnp.exp2(a_cs_t[:, Q - 1:Q] - a_cs_t)
        ea = jnp.exp2(a_cs)
        dec = ea[Q - 1:Q, :]
        for g in range(SSM_GROUPS):
            bsl = slice(D_INNER + g * SSM_STATE, D_INNER + (g + 1) * SSM_STATE)
            csl = slice(D_INNER + SSM_GROUPS * SSM_STATE + g * SSM_STATE,
                        D_INNER + SSM_GROUPS * SSM_STATE + (g + 1) * SSM_STATE)
            b_t = xc[rs, bsl].T
            c_g = xc[rs, csl].astype(BF16)
            cb = _dot(c_g, b_t.astype(BF16))
            gsl = slice(g * SSM_GROUP_W, (g + 1) * SSM_GROUP_W)
            y_off = _dot(c_g, st_ref[:, gsl].astype(BF16))
            for j in range(SSM_GROUP_W // LANES):
                h0 = (g * SSM_GROUP_W + j * LANES) // SSM_HEAD_DIM
                lsl = slice(g * SSM_GROUP_W + j * LANES, g * SSM_GROUP_W + (j + 1) * LANES)
                xs = xc[rs, lsl]
                xs_b = xs.astype(BF16)
                zero_b = jnp.zeros_like(xs_b)
                wmat = jnp.concatenate([jnp.where(lo_tile, xs_b, zero_b), jnp.where(lo_tile, zero_b, xs_b)],
                                       axis=0)
                tops, bots = [], []
                for hh in (h0, h0 + 1):
                    seg = a_cs[:, hh:hh + 1] - a_cs_t[hh:hh + 1, :]
                    decay = jnp.exp2(jnp.where(causal, seg, neg_inf))
                    tops.append(cb * decay * dt_t[hh:hh + 1, :])
                    bots.append(b_t * w_t[hh:hh + 1, :])
                lhs = jnp.concatenate([jnp.concatenate(tops, axis=1), jnp.concatenate(bots, axis=1)],
                                      axis=0).astype(BF16)
                r = _dot(lhs, wmat)
                escale = jnp.where(lo_lane, ea[:, h0:h0 + 1], ea[:, h0 + 1:h0 + 2])
                y_ref[rs, lsl] = (r[0:Q, :] + y_off[:, j * LANES:(j + 1) * LANES] * escale
                                  + dskip_ref[:, lsl] * xs)
                dscale = jnp.where(lo_lane, dec[:, h0:h0 + 1], dec[:, h0 + 1:h0 + 2])
                st_ref[:, lsl] = st_ref[:, lsl] * dscale + r[Q:2 * Q, :]

    yg = y_ref[...] * _silu(_dot_nt(h, wt_ref[0:D_INNER, :]))
    out = jnp.zeros((tile, D_MODEL), F32)
    for g in range(SSM_GROUPS):
        gsl = slice(g * SSM_GROUP_W, (g + 1) * SSM_GROUP_W)
        ygg = yg[:, gsl]
        yn = ygg * lax.rsqrt(jnp.mean(ygg * ygg, axis=-1, keepdims=True) + EPS) * nw_ref[:, gsl]
        out = out + _dot(yn.astype(BF16), w_out_ref[gsl, :])
    xo_ref[0] = x + out

    @pl.when(t == pl.num_programs(1) - 1)
    def _():
        sconv_ref[0] = xext_ref[tile:tile + HALO, :]
        state_ref[0] = st_ref[...].T


def _odd_prompt(x, consts, tile):
    B, L, D = x.shape
    kern = functools.partial(_odd_prompt_kernel, tile=tile)
    arrays, specs = _operands(consts)
    return pl.pallas_call(
        kern,
        grid=(B, L // tile),
        in_specs=[pl.BlockSpec((1, tile, D), lambda b, t: (b, t, 0))] + specs,
        out_specs=[pl.BlockSpec((1, tile, D), lambda b, t: (b, t, 0)),
                   pl.BlockSpec((1, HALO, SSM_CONV_DIM), lambda b, t: (b, 0, 0)),
                   pl.BlockSpec((1, D_INNER, SSM_STATE), lambda b, t: (b, 0, 0))],
        out_shape=[jax.ShapeDtypeStruct((B, L, D), F32),
                   jax.ShapeDtypeStruct((B, HALO, SSM_CONV_DIM), F32),
                   jax.ShapeDtypeStruct((B, D_INNER, SSM_STATE), F32)],
        scratch_shapes=[pltpu.VMEM((HALO + tile, SSM_CONV_DIM), F32),
                        pltpu.VMEM((SSM_STATE, D_INNER), F32),
                        pltpu.VMEM((tile, D_INNER), F32)],
        compiler_params=_params(("arbitrary", "arbitrary")),
        name="odd_prompt",
    )(x, *arrays)


def _finish_xattn(x, o8_ref, wo_ref):
    for r, j in enumerate(_head_rows_order()):
        x = x + _dot(o8_ref[:, r, :].astype(BF16), wo_ref[j * LANES:(j + 1) * LANES, :])
    return x


def _emit_query(x, g_ref, wq_ref, q8_ref):
    q = _dot(_rmsnorm(x, g_ref[...]).astype(BF16), wq_ref[...]) * (XA_HEAD_DIM ** -0.5 * LOG2E)
    for r, j in enumerate(_head_rows_order()):
        q8_ref[:, r, :] = q[:, j * LANES:(j + 1) * LANES]


def _even_sample_kernel(*refs, has_prev):
    refs = list(refs)
    x_ref = refs.pop(0)
    x = x_ref[...]
    if has_prev:
        o8_ref, wo_ref = refs.pop(0), refs.pop(0)
        x = _finish_xattn(x, o8_ref, wo_ref)
    (g_ref, w_in_ref, pw_ref, ps_ref, cw_ref, w_out_ref, pool_ref, cst_ref, gq_ref, wq_ref,
     xo_ref, q8_ref, pool_o_ref, cst_o_ref) = refs
    D = D_MODEL
    h = _rmsnorm(x, g_ref[...]).astype(BF16)

    def proj(j):
        return _dot(h, w_in_ref[:, j * D:(j + 1) * D])

    u = proj(0)
    g_pool = proj(1)
    out = jnp.zeros_like(x)
    for g, w in enumerate(POOL_WINDOWS):
        sl = slice(g * POOL_GROUP, (g + 1) * POOL_GROUP)
        s = u[:, sl]
        for j in range(POOL_BUF - (w - 1), POOL_BUF):
            s = s + pool_ref[j, :, sl]
        cnt = float(min(PAST_LEN + 1, w))
        diff = s / cnt - u[:, sl]
        yp = _dot(diff.astype(BF16), pw_ref[g]) * ps_ref[:, sl] * _silu(g_pool[:, sl])
        out = out + _dot(yp.astype(BF16), w_out_ref[sl, :])
    for j in range(POOL_BUF - 1):
        pool_o_ref[j] = pool_ref[j + 1]
    pool_o_ref[POOL_BUF - 1] = u

    b_gate = proj(2)
    cv = proj(3) * proj(4)
    conv = cw_ref[0:1, :] * cst_ref[:, 0, :] + cw_ref[1:2, :] * cst_ref[:, 1, :] + cw_ref[2:3, :] * cv
    cst_o_ref[:, 0, :] = cst_ref[:, 1, :]
    cst_o_ref[:, 1, :] = cv
    yc = b_gate * conv * _silu(proj(5))
    out = out + _dot(yc.astype(BF16), w_out_ref[D:2 * D, :])
    x = x + out
    xo_ref[...] = x
    _emit_query(x, gq_ref, wq_ref, q8_ref)


class _StackedOut:
    def __init__(self, shape, layer, prev):
        self.shape, self.layer, self.prev = shape, layer, prev


def _sample_call(kern, ops, outs, name):
    args, specs = _operands(ops)
    n_in = len(args)
    out_specs, out_shapes, aliases = [], [], {}
    for k, o in enumerate(outs):
        if isinstance(o, _StackedOut):
            nd = len(o.shape) - 1
            out_specs.append(pl.BlockSpec((None,) + o.shape[1:], lambda *_, l=o.layer, nd=nd: (l,) + (0,) * nd))
            out_shapes.append(jax.ShapeDtypeStruct(o.shape, F32))
            if o.prev is not None:
                aliases[len(args)] = k
                args.append(o.prev)
                specs.append(pl.BlockSpec(memory_space=pl.ANY))
        else:
            out_specs.append(_const_spec(o))
            out_shapes.append(jax.ShapeDtypeStruct(o, F32))
    n_alias = len(args) - n_in

    def body(*refs):
        kern(*refs[:n_in], *refs[n_in + n_alias:])

    return pl.pallas_call(
        body,
        grid=(1,),
        in_specs=specs,
        out_specs=out_specs,
        out_shape=out_shapes,
        input_output_aliases=aliases,
        compiler_params=_params(("arbitrary",)),
        name=name,
    )(*args)


def _slab_shape(rows):
    return (rows, D_MODEL // LANES, LANES)


def _even_sample(x, prev, consts, pool, pool_out, cst, q_consts):
    ops = [x] + (list(prev) if prev else []) + list(consts) + [pool, cst] + list(q_consts)
    kern = functools.partial(_even_sample_kernel, has_prev=prev is not None)
    outs = [x.shape, _slab_shape(x.shape[0]), _StackedOut(pool[0].shape, pool[1], pool_out), cst[0].shape[1:]]
    return _sample_call(kern, ops, outs, "even_sample")


def _xattn_sample_stream_kernel(q_ref, k_ref, v_ref, o_ref, *, bt):
    slab_rows, lanes = q_ref.shape[1:]
    ones = jnp.ones((lanes, lanes), BF16)
    for b in range(bt):
        prod = (k_ref[b] * q_ref[b][None]).reshape(N_MEM * slab_rows, lanes)
        r = _dot(prod.astype(BF16), ones).reshape(N_MEM, slab_rows, lanes)
        s = r + pltpu.roll(r, XA_HEADS, 1)
        e = jnp.exp2(s - jnp.max(s, axis=0, keepdims=True))
        o_ref[b] = jnp.sum(e * v_ref[b], axis=0) / jnp.sum(e, axis=0)


def _xattn_sample_stream(q8, k8, v8, layer, bt):
    DB = q8.shape[0]
    slab = q8.shape[1:]
    kern = functools.partial(_xattn_sample_stream_kernel, bt=bt)
    return pl.pallas_call(
        kern,
        grid=(DB // bt,),
        in_specs=[pl.BlockSpec((bt,) + slab, lambda i: (i, 0, 0)),
                  pl.BlockSpec((None, bt, N_MEM) + slab, lambda i: (layer, i, 0, 0, 0)),
                  pl.BlockSpec((None, bt, N_MEM) + slab, lambda i: (layer, i, 0, 0, 0))],
        out_specs=pl.BlockSpec((bt,) + slab, lambda i: (i, 0, 0)),
        out_shape=jax.ShapeDtypeStruct(q8.shape, F32),
        compiler_params=_params(("arbitrary",)),
        name="xattn_sample_stream",
    )(q8, k8, v8)


def _xattn_sample_final_kernel(x_ref, o8_ref, wo_ref, gf_ref, xo_ref):
    xo_ref[...] = _rmsnorm(_finish_xattn(x_ref[...], o8_ref, wo_ref), gf_ref[...])


def _xattn_sample_final(x, o8, wo, gf):
    return _sample_call(_xattn_sample_final_kernel, (x, o8, wo, gf), [x.shape], "xattn_sample_final")[0]


def _odd_sample_in_kernel(x_ref, o8_ref, wo_ref, g_ref, wt_ref, cw_ref, cb_ref, dtb_ref, alog_ref,
                          sconv_ref, x1_ref, z_ref, xs_ref, bm_ref, cm_ref, xdt_ref, dec_ref, sconv_o_ref):
    x = _finish_xattn(x_ref[...], o8_ref, wo_ref)
    x1_ref[...] = x
    h = _rmsnorm(x, g_ref[...]).astype(BF16)
    xbc = _dot_nt(h, wt_ref[D_INNER:D_INNER + SSM_CONV_DIM, :])
    conv = cw_ref[3:4, :] * xbc
    for k in range(SSM_CONV_K - 1):
        conv = conv + cw_ref[k:k + 1, :] * sconv_ref[k]
    for k in range(SSM_CONV_K - 2):
        sconv_o_ref[k] = sconv_ref[k + 1]
    sconv_o_ref[SSM_CONV_K - 2] = xbc
    xc = _silu(conv + cb_ref[...])
    xs = xc[:, :D_INNER]
    dt = _softplus(_dt_raw(h, wt_ref) + dtb_ref[...])
    dta = dt * (-jnp.exp(alog_ref[...]))
    hrow = lax.broadcasted_iota(jnp.int32, (LANES, D_INNER), 0)
    ccol = lax.broadcasted_iota(jnp.int32, (LANES, D_INNER), 1)
    expand = jnp.where(ccol // SSM_HEAD_DIM == hrow, 1.0, 0.0).astype(BF16)
    z_ref[...] = _dot_nt(h, wt_ref[0:D_INNER, :])
    xs_ref[...] = xs
    bm_ref[...] = xc[:, D_INNER:D_INNER + SSM_GROUPS * SSM_STATE]
    cm_ref[...] = xc[:, D_INNER + SSM_GROUPS * SSM_STATE:]
    xdt_ref[...] = xs * _dot_exact_lhs(dt, expand)
    dec_ref[...] = jnp.exp(dta)


def _odd_sample_in(x, prev, consts, sconv, sconv_out):
    DB = x.shape[0]
    ops = [x] + list(prev) + list(consts) + [sconv]
    outs = [x.shape, (DB, D_INNER), (DB, D_INNER), (DB, SSM_GROUPS * SSM_STATE), (DB, SSM_GROUPS * SSM_STATE),
            (DB, D_INNER), (DB, LANES), _StackedOut(sconv[0].shape, sconv[1], sconv_out)]
    return _sample_call(_odd_sample_in_kernel, ops, outs, "odd_sample_in")


def _odd_sample_state_kernel(dec_ref, xdt_ref, bm_ref, cm_ref, s_ref, *rest, bt):
    so_ref, y_ref = rest[-2:]
    i = pl.program_id(0)
    pad = jnp.zeros((LANES - bt, D_INNER), F32)
    xdt_t = jnp.concatenate([xdt_ref[...], pad], axis=0).T.astype(BF16)
    padn = jnp.zeros((LANES - bt, SSM_STATE), F32)
    row = lax.broadcasted_iota(jnp.int32, (LANES, SSM_STATE), 0)
    y_acc = [jnp.zeros((SSM_GROUP_W, LANES), F32) for _ in range(SSM_GROUPS)]
    heads_per_group = SSM_GROUP_W // SSM_HEAD_DIM
    for g in range(SSM_GROUPS):
        nsl = slice(g * SSM_STATE, (g + 1) * SSM_STATE)
        gsl = slice(g * SSM_GROUP_W, (g + 1) * SSM_GROUP_W)
        b_all = jnp.concatenate([bm_ref[:, nsl], padn], axis=0)
        c_all = jnp.concatenate([cm_ref[:, nsl], padn], axis=0)
        for b in range(bt):
            b_sel = jnp.where(row == b, b_all, 0.0).astype(BF16)
            c_sel = jnp.where(row == b, c_all, 0.0).astype(BF16)
            upd = _dot(xdt_t[gsl, :], b_sel)
            pieces = []
            for e in range(heads_per_group):
                hd = g * heads_per_group + e
                rsl = slice(hd * SSM_HEAD_DIM, (hd + 1) * SSM_HEAD_DIM)
                new = s_ref[b, rsl, :] * dec_ref[i * bt + b, hd] + upd[e * SSM_HEAD_DIM:(e + 1) * SSM_HEAD_DIM, :]
                so_ref[b, rsl, :] = new
                pieces.append(new)
            y_acc[g] = y_acc[g] + _dot_nt(jnp.concatenate(pieces, axis=0).astype(BF16), c_sel)
    y_ref[...] = jnp.concatenate(y_acc, axis=0).T[0:bt, :]


def _odd_sample_state(dec, xdt, bm, cm, state_all, layer, prev_out, bt):
    DB = xdt.shape[0]
    kern = functools.partial(_odd_sample_state_kernel, bt=bt)
    gn = SSM_GROUPS * SSM_STATE
    in_specs = [pl.BlockSpec(memory_space=pltpu.SMEM),
                pl.BlockSpec((bt, D_INNER), lambda i: (i, 0)),
                pl.BlockSpec((bt, gn), lambda i: (i, 0)),
                pl.BlockSpec((bt, gn), lambda i: (i, 0)),
                pl.BlockSpec((None, bt, D_INNER, SSM_STATE), lambda i: (layer, i, 0, 0))]
    args = [dec, xdt, bm, cm, state_all]
    aliases = {}
    if prev_out is not None:
        in_specs.append(pl.BlockSpec(memory_space=pl.ANY))
        args.append(prev_out)
        aliases = {len(args) - 1: 0}
    return pl.pallas_call(
        kern,
        grid=(DB // bt,),
        in_specs=in_specs,
        out_specs=[pl.BlockSpec((None, bt, D_INNER, SSM_STATE), lambda i: (layer, i, 0, 0)),
                   pl.BlockSpec((bt, D_INNER), lambda i: (i, 0))],
        out_shape=[jax.ShapeDtypeStruct(state_all.shape, F32), jax.ShapeDtypeStruct((DB, D_INNER), F32)],
        input_output_aliases=aliases,
        compiler_params=_params(("arbitrary",)),
        name="odd_sample_state",
    )(*args)


def _odd_sample_out_kernel(x_ref, y_ref, xs_ref, z_ref, dskip_ref, nw_ref, w_out_ref, gq_ref, wq_ref,
                           xo_ref, q8_ref):
    yg = (y_ref[...] + dskip_ref[...] * xs_ref[...]) * _silu(z_ref[...])
    out = jnp.zeros(x_ref.shape, F32)
    for g in range(SSM_GROUPS):
        gsl = slice(g * SSM_GROUP_W, (g + 1) * SSM_GROUP_W)
        ygg = yg[:, gsl]
        yn = ygg * lax.rsqrt(jnp.mean(ygg * ygg, axis=-1, keepdims=True) + EPS) * nw_ref[:, gsl]
        out = out + _dot(yn.astype(BF16), w_out_ref[gsl, :])
    x = x_ref[...] + out
    xo_ref[...] = x
    _emit_query(x, gq_ref, wq_ref, q8_ref)


def _odd_sample_out(x, y, xs, z, consts, q_consts):
    ops = [x, y, xs, z] + list(consts) + list(q_consts)
    return _sample_call(_odd_sample_out_kernel, ops, [x.shape, _slab_shape(x.shape[0])], "odd_sample_out")


def _pad_lanes(v):
    return jnp.pad(v, ((0, 0), (0, LANES - v.shape[-1])))


def kernel(x_prompt, x_sample, mem_prompt, cache_mem_k, cache_mem_v, state_pool, state_conv, state_ssm_conv,
           state_ssm, norm_mix, norm_xa, norm_final, w_in_even, pool_w, pool_scale, conv_w, w_out_even, w_in_odd,
           ssm_conv_w, ssm_conv_b, dt_bias, a_log, d_skip, ssm_norm, w_out_odd, w_xa_q, w_xa_k, w_xa_v, w_xa_o):
    B, L, D = x_prompt.shape
    DB = x_sample.shape[0]
    depth = norm_mix.shape[0]
    tile = min(L, 512)
    tile_odd = min(L, 512)
    tile_xa = min(L, 1024)
    bt_x = min(DB, 8)
    bt_s = min(DB, 8)

    mem_k, mem_v, mem_k8, mem_v8 = _kv_proj(mem_prompt.reshape(B * N_MEM, D), w_xa_k, w_xa_v)
    mem_k = mem_k.reshape(depth, B, N_MEM, D)
    mem_v = mem_v.reshape(depth, B, N_MEM, D)
    slab = (D // LANES, LANES)
    cache_k8 = _to_head_slabs(cache_mem_k)
    cache_v8 = _to_head_slabs(cache_mem_v)
    ssm_all = state_ssm.reshape(state_ssm.shape[0], DB, D_INNER, SSM_STATE)
    ssm_s = None

    def rows(a):
        return a.reshape(a.shape[0], 1, a.shape[-1])

    g_mix_all, g_xa_all = rows(norm_mix), rows(norm_xa)
    w_in_e, pw_e, ps_e = w_in_even.astype(BF16), pool_w.astype(BF16), rows(pool_scale)
    w_out_e = w_out_even.astype(BF16)
    wt_o = jnp.swapaxes(w_in_odd, 1, 2).astype(BF16)
    cb_o, dtb_o, alog_o = rows(ssm_conv_b), rows(_pad_lanes(dt_bias)), rows(_pad_lanes(a_log))
    dskip_o, nw_o = rows(jnp.repeat(d_skip, SSM_HEAD_DIM, axis=1)), rows(ssm_norm)
    w_out_o = w_out_odd.astype(BF16)
    wq_all, wo_all = w_xa_q.astype(BF16), w_xa_o.astype(BF16)

    xp = x_prompt
    xs_ = x_sample.reshape(DB, D)
    gf = norm_final.reshape(1, D)
    pool_p, conv_p, conv_s, sconv_p, ssm_p = [], [], [], [], []
    pool_t, sconv_t = state_pool.transpose(0, 2, 1, 3), state_ssm_conv.transpose(0, 2, 1, 3)
    pool_s = sconv_s = None
    prev = None
    for l in range(depth):
        i = l // 2
        g_mix = _layer(g_mix_all, l)
        g_xa, wq, wo = _layer(g_xa_all, l), _layer(wq_all, l), _layer(wo_all, l)
        if l % 2 == 0:
            consts = [g_mix, _layer(w_in_e, i), _layer(pw_e, i), _layer(ps_e, i), _layer(conv_w, i),
                      _layer(w_out_e, i)]
            xp, pb, cbuf = _even_prompt(xp, consts, tile)
            pool_p.append(pb[:, HALO - POOL_BUF:, :])
            conv_p.append(cbuf[:, HALO - (CONV_K - 1):, :])
            xs_, q8, pool_s, cst_n = _even_sample(xs_, prev, consts, _layer(pool_t, i), pool_s,
                                                  _layer(state_conv, i), (g_xa, wq))
            conv_s.append(cst_n)
        else:
            in_consts = [g_mix, _layer(wt_o, i), _layer(ssm_conv_w, i),
                         _layer(cb_o, i), _layer(dtb_o, i), _layer(alog_o, i)]
            out_consts = [_layer(dskip_o, i), _layer(nw_o, i), _layer(w_out_o, i)]
            xp, sb, st = _odd_prompt(xp, in_consts + out_consts, tile_odd)
            sconv_p.append(sb[:, HALO - (SSM_CONV_K - 1):, :])
            ssm_p.append(st.reshape(B, SSM_HEADS, SSM_HEAD_DIM, SSM_STATE))
            xs_, z, xs_in, bm, cm, xdt, dec, sconv_s = _odd_sample_in(xs_, prev, in_consts,
                                                                      _layer(sconv_t, i), sconv_s)
            ssm_s, y = _odd_sample_state(dec, xdt, bm, cm, ssm_all, i, ssm_s, bt_s)
            xs_, q8 = _odd_sample_out(xs_, y, xs_in, z, out_consts, (g_xa, wq))
        final = l == depth - 1
        xp = _xattn_prompt(xp, [g_xa, wq, wo], mem_k, mem_v, l, gf, tile_xa, final)
        prev = (_xattn_sample_stream(q8, cache_k8, cache_v8, l, bt_x), wo)
    xs_ = _xattn_sample_final(xs_, prev[0], prev[1], gf)

    mem_shape = (depth, B, N_MEM) + slab
    return (xp, xs_.reshape(DB, 1, D),
            _from_head_slabs(mem_k8.reshape(mem_shape)), _from_head_slabs(mem_v8.reshape(mem_shape)),
            jnp.stack(pool_p), pool_s.transpose(0, 2, 1, 3), jnp.stack(conv_p), jnp.stack(conv_s),
            jnp.stack(sconv_p), sconv_s.transpose(0, 2, 1, 3), jnp.stack(ssm_p),
            ssm_s.reshape(ssm_all.shape[0], DB, SSM_HEADS, SSM_HEAD_DIM, SSM_STATE))
```

```python
import functools

import jax
import jax.numpy as jnp
from jax import lax
from jax.experimental import pallas as pl
from jax.experimental.pallas import tpu as pltpu

F32 = jnp.float32
BF16 = jnp.bfloat16

D_MODEL = 1024
PAST_LEN = 16384
POOL_WINDOWS = (2, 4, 8, 16)
POOL_GROUP = D_MODEL // len(POOL_WINDOWS)
POOL_BUF = max(POOL_WINDOWS) - 1
CONV_K = 3
D_INNER = 2 * D_MODEL
SSM_HEAD_DIM = 64
SSM_HEADS = D_INNER // SSM_HEAD_DIM
SSM_GROUPS = 4
SSM_STATE = 128
SSM_GROUP_W = D_INNER // SSM_GROUPS
SSM_CONV_K = 4
SSM_CONV_DIM = D_INNER + 2 * SSM_GROUPS * SSM_STATE
SSM_CHUNK = 128
N_MEM = 256
XA_HEADS = 4
XA_HEAD_DIM = D_MODEL // XA_HEADS
EPS = 1e-6
LOG2E = 1.4426950408889634
LANES = 128
HALO = 16
VMEM_LIMIT = 56 * 1024 * 1024


def _rmsnorm(x, g):
    return x * lax.rsqrt(jnp.mean(x * x, axis=-1, keepdims=True) + EPS) * g


def _silu(x):
    return x * jax.nn.sigmoid(x)


def _softplus(x):
    return jnp.maximum(x, 0.0) + jnp.log1p(jnp.exp(-jnp.abs(x)))


def _dot(a, b):
    return jnp.dot(a, b, preferred_element_type=F32)


def _dot_nt(a, b):
    return lax.dot_general(a, b, (((1,), (1,)), ((), ())), preferred_element_type=F32)


def _dt_raw(h, wt_ref):
    raw = _dot_nt(h, wt_ref[D_INNER + SSM_CONV_DIM:, :])
    return jnp.concatenate([raw, jnp.zeros((raw.shape[0], LANES - SSM_HEADS), F32)], axis=1)


def _split3(x):
    hi = x.astype(BF16)
    r1 = x - hi.astype(F32)
    mid = r1.astype(BF16)
    lo = (r1 - mid.astype(F32)).astype(BF16)
    return hi, mid, lo


def _dot_exact_rhs(a_bf16, x):
    hi, mid, lo = _split3(x)
    return _dot(a_bf16, hi) + _dot(a_bf16, mid) + _dot(a_bf16, lo)


def _dot_exact_lhs(x, b_bf16):
    hi, mid, lo = _split3(x)
    return _dot(hi, b_bf16) + _dot(mid, b_bf16) + _dot(lo, b_bf16)


def _const_spec(shape):
    nd = len(shape)
    return pl.BlockSpec(shape, lambda *_: (0,) * nd, pipeline_mode=pl.Buffered(1))


def _layer(arr, l):
    return (arr, l)


def _operands(ops):
    arrays, specs = [], []
    for op in ops:
        if isinstance(op, tuple):
            arr, l = op
            nd = arr.ndim - 1
            specs.append(pl.BlockSpec((None,) + arr.shape[1:], lambda *_, l=l, nd=nd: (l,) + (0,) * nd,
                                      pipeline_mode=pl.Buffered(1)))
        else:
            arr = op
            specs.append(_const_spec(arr.shape))
        arrays.append(arr)
    return arrays, specs


def _params(sem):
    return pltpu.CompilerParams(dimension_semantics=sem, vmem_limit_bytes=VMEM_LIMIT)


def _head_rows_order():
    lane_tiles = XA_HEAD_DIM // LANES
    return [(r % XA_HEADS) * lane_tiles + r // XA_HEADS for r in range(XA_HEADS * lane_tiles)]


def _to_head_slabs(a):
    lead = a.shape[:-2]
    lane_tiles = XA_HEAD_DIM // LANES
    nd = len(lead)
    a = a.reshape(lead + (XA_HEADS, lane_tiles, LANES))
    a = a.transpose(tuple(range(nd)) + (nd + 1, nd, nd + 2))
    return a.reshape(lead + (XA_HEADS * lane_tiles, LANES))


def _from_head_slabs(a):
    lead = a.shape[:-2]
    lane_tiles = XA_HEAD_DIM // LANES
    nd = len(lead)
    a = a.reshape(lead + (lane_tiles, XA_HEADS, LANES))
    a = a.transpose(tuple(range(nd)) + (nd + 1, nd, nd + 2))
    return a.reshape(lead + (XA_HEADS, XA_HEAD_DIM))


def _kv_proj_kernel(mem_ref, wk_ref, wv_ref, k_ref, v_ref, k8_ref, v8_ref):
    m = mem_ref[...].astype(BF16)
    k = _dot(m, wk_ref[0].astype(BF16))
    v = _dot(m, wv_ref[0].astype(BF16))
    k_ref[0] = k
    v_ref[0] = v
    for r, j in enumerate(_head_rows_order()):
        k8_ref[0, :, r, :] = k[:, j * LANES:(j + 1) * LANES]
        v8_ref[0, :, r, :] = v[:, j * LANES:(j + 1) * LANES]


def _kv_proj(mem2d, w_k, w_v):
    rows = mem2d.shape[0]
    depth = w_k.shape[0]
    tr = min(rows, 512)
    out = jax.ShapeDtypeStruct((depth, rows, D_MODEL), F32)
    out8 = jax.ShapeDtypeStruct((depth, rows, D_MODEL // LANES, LANES), F32)
    return pl.pallas_call(
        _kv_proj_kernel,
        grid=(depth, rows // tr),
        in_specs=[pl.BlockSpec((tr, D_MODEL), lambda l, r: (r, 0)),
                  pl.BlockSpec((1, D_MODEL, D_MODEL), lambda l, r: (l, 0, 0)),
                  pl.BlockSpec((1, D_MODEL, D_MODEL), lambda l, r: (l, 0, 0))],
        out_specs=[pl.BlockSpec((1, tr, D_MODEL), lambda l, r: (l, r, 0)),
                   pl.BlockSpec((1, tr, D_MODEL), lambda l, r: (l, r, 0)),
                   pl.BlockSpec((1, tr, D_MODEL // LANES, LANES), lambda l, r: (l, r, 0, 0)),
                   pl.BlockSpec((1, tr, D_MODEL // LANES, LANES), lambda l, r: (l, r, 0, 0))],
        out_shape=[out, out, out8, out8],
        compiler_params=_params(("arbitrary", "arbitrary")),
        name="kv_proj",
    )(mem2d, w_k, w_v)


def _even_prompt_kernel(x_ref, g_ref, w_in_ref, pw_ref, ps_ref, cw_ref, w_out_ref,
                        xo_ref, pool_ref, conv_ref, ext_ref, cext_ref, *, tile):
    t = pl.program_id(1)
    D = D_MODEL

    @pl.when(t == 0)
    def _():
        ext_ref[0:HALO, :] = jnp.zeros((HALO, D), F32)
        cext_ref[0:HALO, :] = jnp.zeros((HALO, D), F32)

    @pl.when(t > 0)
    def _():
        ext_ref[0:HALO, :] = ext_ref[tile:tile + HALO, :]
        cext_ref[0:HALO, :] = cext_ref[tile:tile + HALO, :]

    x = x_ref[0]
    h = _rmsnorm(x, g_ref[...]).astype(BF16)

    def proj(j):
        return _dot(h, w_in_ref[:, j * D:(j + 1) * D])

    u = proj(0)
    ext_ref[HALO:HALO + tile, :] = u
    g_pool = proj(1)
    pos = t * tile + lax.broadcasted_iota(jnp.int32, (tile, POOL_GROUP), 0)
    mixed = []
    for g, w in enumerate(POOL_WINDOWS):
        sl = slice(g * POOL_GROUP, (g + 1) * POOL_GROUP)
        s = ext_ref[:, sl]
        k = 1
        while k < w:
            s = s + pltpu.roll(s, k, 0)
            k *= 2
        cnt = jnp.minimum(pos + 1, w).astype(F32)
        diff = s[HALO:, :] / cnt - u[:, sl]
        yp = _dot(diff.astype(BF16), pw_ref[g]) * ps_ref[:, sl] * _silu(g_pool[:, sl])
        mixed.append(yp.astype(BF16))

    b_gate = proj(2)
    cv = proj(3) * proj(4)
    cext_ref[HALO:HALO + tile, :] = cv
    ce = cext_ref[...]
    conv = cw_ref[2:3, :] * cv
    conv = conv + cw_ref[1:2, :] * pltpu.roll(ce, 1, 0)[HALO:, :]
    conv = conv + cw_ref[0:1, :] * pltpu.roll(ce, 2, 0)[HALO:, :]
    yc = b_gate * conv * _silu(proj(5))
    mixed.append(yc.astype(BF16))
    xo_ref[0] = x + _dot(jnp.concatenate(mixed, axis=1), w_out_ref[...])

    @pl.when(t == pl.num_programs(1) - 1)
    def _():
        pool_ref[0] = ext_ref[tile:tile + HALO, :]
        conv_ref[0] = cext_ref[tile:tile + HALO, :]


def _even_prompt(x, consts, tile):
    B, L, D = x.shape
    kern = functools.partial(_even_prompt_kernel, tile=tile)
    arrays, specs = _operands(consts)
    return pl.pallas_call(
        kern,
        grid=(B, L // tile),
        in_specs=[pl.BlockSpec((1, tile, D), lambda b, t: (b, t, 0))] + specs,
        out_specs=[pl.BlockSpec((1, tile, D), lambda b, t: (b, t, 0)),
                   pl.BlockSpec((1, HALO, D), lambda b, t: (b, 0, 0)),
                   pl.BlockSpec((1, HALO, D), lambda b, t: (b, 0, 0))],
        out_shape=[jax.ShapeDtypeStruct((B, L, D), F32),
                   jax.ShapeDtypeStruct((B, HALO, D), F32),
                   jax.ShapeDtypeStruct((B, HALO, D), F32)],
        scratch_shapes=[pltpu.VMEM((HALO + tile, D), F32), pltpu.VMEM((HALO + tile, D), F32)],
        compiler_params=_params(("arbitrary", "arbitrary")),
        name="even_prompt",
    )(x, *arrays)


def _xattn_prompt_kernel(x_ref, g_ref, wq_ref, wo_ref, k_ref, v_ref, gf_ref, xo_ref, *, final):
    x = x_ref[0]
    h = _rmsnorm(x, g_ref[...]).astype(BF16)
    q = (_dot(h, wq_ref[...]) * (XA_HEAD_DIM ** -0.5 * LOG2E)).astype(BF16)
    heads = []
    for hd in range(XA_HEADS):
        sl = slice(hd * XA_HEAD_DIM, (hd + 1) * XA_HEAD_DIM)
        kh = k_ref[0, :, sl].astype(BF16)
        vh = v_ref[0, :, sl].astype(BF16)
        s = _dot_nt(q[:, sl], kh)
        e = jnp.exp2(s - jnp.max(s, axis=-1, keepdims=True))
        o = _dot(e.astype(BF16), vh) / jnp.sum(e, axis=-1, keepdims=True)
        heads.append(o.astype(BF16))
    y = x + _dot(jnp.concatenate(heads, axis=1), wo_ref[...])
    if final:
        y = _rmsnorm(y, gf_ref[...])
    xo_ref[0] = y


def _xattn_prompt(x, consts, k, v, layer, gf, tile, final):
    B, L, D = x.shape
    kern = functools.partial(_xattn_prompt_kernel, final=final)
    arrays, specs = _operands(consts)
    return pl.pallas_call(
        kern,
        grid=(B, L // tile),
        in_specs=[pl.BlockSpec((1, tile, D), lambda b, t: (b, t, 0))] + specs + [
            pl.BlockSpec((None, 1, N_MEM, D), lambda b, t: (layer, b, 0, 0)),
            pl.BlockSpec((None, 1, N_MEM, D), lambda b, t: (layer, b, 0, 0)),
            _const_spec(gf.shape)],
        out_specs=pl.BlockSpec((1, tile, D), lambda b, t: (b, t, 0)),
        out_shape=jax.ShapeDtypeStruct((B, L, D), F32),
        compiler_params=_params(("arbitrary", "arbitrary")),
        name="xattn_prompt",
    )(x, *arrays, k, v, gf)


def _odd_prompt_kernel(x_ref, g_ref, wt_ref, cw_ref, cb_ref, dtb_ref, alog_ref,
                       dskip_ref, nw_ref, w_out_ref,
                       xo_ref, sconv_ref, state_ref, xext_ref, st_ref, y_ref, *, tile):
    t = pl.program_id(1)
    Q = SSM_CHUNK
    C = SSM_CONV_DIM

    @pl.when(t == 0)
    def _():
        xext_ref[0:HALO, :] = jnp.zeros((HALO, C), F32)
        st_ref[...] = jnp.zeros(st_ref.shape, F32)

    @pl.when(t > 0)
    def _():
        xext_ref[0:HALO, :] = xext_ref[tile:tile + HALO, :]

    x = x_ref[0]
    h = _rmsnorm(x, g_ref[...]).astype(BF16)
    xbc = _dot_nt(h, wt_ref[D_INNER:D_INNER + SSM_CONV_DIM, :])
    xext_ref[HALO:HALO + tile, :] = xbc
    xe = xext_ref[...]
    xe2 = pltpu.roll(xe, 2, 0)
    odd_taps = cw_ref[2:3, :] * xe + cw_ref[0:1, :] * xe2
    conv = cw_ref[3:4, :] * xbc + cw_ref[1:2, :] * xe2[HALO:, :] + pltpu.roll(odd_taps, 1, 0)[HALO:, :]
    xc = _silu(conv + cb_ref[...])
    dt = _softplus(_dt_raw(h, wt_ref) + dtb_ref[...])
    dta = dt * (-jnp.exp(alog_ref[...]) * LOG2E)

    row = lax.broadcasted_iota(jnp.int32, (Q, Q), 0)
    col = lax.broadcasted_iota(jnp.int32, (Q, Q), 1)
    causal = row >= col
    tri = jnp.where(causal, 1.0, 0.0).astype(BF16)
    lo_lane = lax.broadcasted_iota(jnp.int32, (1, LANES), 1) < SSM_HEAD_DIM
    lo_tile = col < SSM_HEAD_DIM
    neg_inf = jnp.float32(-jnp.inf)

    for c in range(tile // Q):
        rs = slice(c * Q, (c + 1) * Q)
        a_cs = _dot_exact_rhs(tri, dta[rs, :])
        a_cs_t = a_cs.T
        dt_t = dt[rs, :].T
        w_t = dt_t * jnp.exp2(a_cs_t[:, Q - 1:Q] - a_cs_t)
        ea = jnp.exp2(a_cs)
        dec = ea[Q - 1:Q, :]
        for g in range(SSM_GROUPS):
            bsl = slice(D_INNER + g * SSM_STATE, D_INNER + (g + 1) * SSM_STATE)
            csl = slice(D_INNER + SSM_GROUPS * SSM_STATE + g * SSM_STATE,
                        D_INNER + SSM_GROUPS * SSM_STATE + (g + 1) * SSM_STATE)
            b_t = xc[rs, bsl].T
            c_g = xc[rs, csl].astype(BF16)
            cb = _dot(c_g, b_t.astype(BF16))
            gsl = slice(g * SSM_GROUP_W, (g + 1) * SSM_GROUP_W)
            y_off = _dot(c_g, st_ref[:, gsl].astype(BF16))
            for j in range(SSM_GROUP_W // LANES):
                h0 = (g * SSM_GROUP_W + j * LANES) // SSM_HEAD_DIM
                lsl = slice(g * SSM_GROUP_W + j * LANES, g * SSM_GROUP_W + (j + 1) * LANES)
                xs = xc[rs, lsl]
                xs_b = xs.astype(BF16)
                zero_b = jnp.zeros_like(xs_b)
                wmat = jnp.concatenate([jnp.where(lo_tile, xs_b, zero_b), jnp.where(lo_tile, zero_b, xs_b)],
                                       axis=0)
                tops, bots = [], []
                for hh in (h0, h0 + 1):
                    seg = a_cs[:, hh:hh + 1] - a_cs_t[hh:hh + 1, :]
                    decay = jnp.exp2(jnp.where(causal, seg, neg_inf))
                    tops.append(cb * decay * dt_t[hh:hh + 1, :])
                    bots.append(b_t * w_t[hh:hh + 1, :])
                lhs = jnp.concatenate([jnp.concatenate(tops, axis=1), jnp.concatenate(bots, axis=1)],
                                      axis=0).astype(BF16)
                r = _dot(lhs, wmat)
                escale = jnp.where(lo_lane, ea[:, h0:h0 + 1], ea[:, h0 + 1:h0 + 2])
                y_ref[rs, lsl] = (r[0:Q, :] + y_off[:, j * LANES:(j + 1) * LANES] * escale
                                  + dskip_ref[:, lsl] * xs)
                dscale = jnp.where(lo_lane, dec[:, h0:h0 + 1], dec[:, h0 + 1:h0 + 2])
                st_ref[:, lsl] = st_ref[:, lsl] * dscale + r[Q:2 * Q, :]

    yg = y_ref[...] * _silu(_dot_nt(h, wt_ref[0:D_INNER, :]))
    normed = []
    for g in range(SSM_GROUPS):
        gsl = slice(g * SSM_GROUP_W, (g + 1) * SSM_GROUP_W)
        ygg = yg[:, gsl]
        yn = ygg * lax.rsqrt(jnp.mean(ygg * ygg, axis=-1, keepdims=True) + EPS) * nw_ref[:, gsl]
        normed.append(yn.astype(BF16))
    xo_ref[0] = x + _dot(jnp.concatenate(normed, axis=1), w_out_ref[...])

    @pl.when(t == pl.num_programs(1) - 1)
    def _():
        sconv_ref[0] = xext_ref[tile:tile + HALO, :]
        state_ref[0] = st_ref[...].T


def _odd_prompt(x, consts, tile):
    B, L, D = x.shape
    kern = functools.partial(_odd_prompt_kernel, tile=tile)
    arrays, specs = _operands(consts)
    return pl.pallas_call(
        kern,
        grid=(B, L // tile),
        in_specs=[pl.BlockSpec((1, tile, D), lambda b, t: (b, t, 0))] + specs,
        out_specs=[pl.BlockSpec((1, tile, D), lambda b, t: (b, t, 0)),
                   pl.BlockSpec((1, HALO, SSM_CONV_DIM), lambda b, t: (b, 0, 0)),
                   pl.BlockSpec((1, D_INNER, SSM_STATE), lambda b, t: (b, 0, 0))],
        out_shape=[jax.ShapeDtypeStruct((B, L, D), F32),
                   jax.ShapeDtypeStruct((B, HALO, SSM_CONV_DIM), F32),
                   jax.ShapeDtypeStruct((B, D_INNER, SSM_STATE), F32)],
        scratch_shapes=[pltpu.VMEM((HALO + tile, SSM_CONV_DIM), F32),
                        pltpu.VMEM((SSM_STATE, D_INNER), F32),
                        pltpu.VMEM((tile, D_INNER), F32)],
        compiler_params=_params(("arbitrary", "arbitrary")),
        name="odd_prompt",
    )(x, *arrays)


def _finish_xattn(x, o8_ref, wo_ref):
    for r, j in enumerate(_head_rows_order()):
        x = x + _dot(o8_ref[:, r, :].astype(BF16), wo_ref[j * LANES:(j + 1) * LANES, :])
    return x


def _emit_query(x, g_ref, wq_ref, q8_ref):
    q = _dot(_rmsnorm(x, g_ref[...]).astype(BF16), wq_ref[...]) * (XA_HEAD_DIM ** -0.5 * LOG2E)
    for r, j in enumerate(_head_rows_order()):
        q8_ref[:, r, :] = q[:, j * LANES:(j + 1) * LANES]


def _even_sample_kernel(*refs, has_prev):
    refs = list(refs)
    x_ref = refs.pop(0)
    x = x_ref[...]
    if has_prev:
        o8_ref, wo_ref = refs.pop(0), refs.pop(0)
        x = _finish_xattn(x, o8_ref, wo_ref)
    (g_ref, w_in_ref, pw_ref, ps_ref, cw_ref, w_out_ref, pool_ref, cst_ref, gq_ref, wq_ref,
     xo_ref, q8_ref, pool_o_ref, cst_o_ref) = refs
    D = D_MODEL
    h = _rmsnorm(x, g_ref[...]).astype(BF16)

    def proj(j):
        return _dot(h, w_in_ref[:, j * D:(j + 1) * D])

    u = proj(0)
    g_pool = proj(1)
    out = jnp.zeros_like(x)
    for g, w in enumerate(POOL_WINDOWS):
        sl = slice(g * POOL_GROUP, (g + 1) * POOL_GROUP)
        s = u[:, sl]
        for j in range(POOL_BUF - (w - 1), POOL_BUF):
            s = s + pool_ref[j, :, sl]
        cnt = float(min(PAST_LEN + 1, w))
        diff = s / cnt - u[:, sl]
        yp = _dot(diff.astype(BF16), pw_ref[g]) * ps_ref[:, sl] * _silu(g_pool[:, sl])
        out = out + _dot(yp.astype(BF16), w_out_ref[sl, :])
    for j in range(POOL_BUF - 1):
        pool_o_ref[j] = pool_ref[j + 1]
    pool_o_ref[POOL_BUF - 1] = u

    b_gate = proj(2)
    cv = proj(3) * proj(4)
    conv = cw_ref[0:1, :] * cst_ref[:, 0, :] + cw_ref[1:2, :] * cst_ref[:, 1, :] + cw_ref[2:3, :] * cv
    cst_o_ref[:, 0, :] = cst_ref[:, 1, :]
    cst_o_ref[:, 1, :] = cv
    yc = b_gate * conv * _silu(proj(5))
    out = out + _dot(yc.astype(BF16), w_out_ref[D:2 * D, :])
    x = x + out
    xo_ref[...] = x
    _emit_query(x, gq_ref, wq_ref, q8_ref)


class _StackedOut:
    def __init__(self, shape, layer, prev):
        self.shape, self.layer, self.prev = shape, layer, prev


def _sample_call(kern, ops, outs, name):
    args, specs = _operands(ops)
    n_in = len(args)
    out_specs, out_shapes, aliases = [], [], {}
    for k, o in enumerate(outs):
        if isinstance(o, _StackedOut):
            nd = len(o.shape) - 1
            out_specs.append(pl.BlockSpec((None,) + o.shape[1:], lambda *_, l=o.layer, nd=nd: (l,) + (0,) * nd))
            out_shapes.append(jax.ShapeDtypeStruct(o.shape, F32))
            if o.prev is not None:
                aliases[len(args)] = k
                args.append(o.prev)
                specs.append(pl.BlockSpec(memory_space=pl.ANY))
        else:
            out_specs.append(_const_spec(o))
            out_shapes.append(jax.ShapeDtypeStruct(o, F32))
    n_alias = len(args) - n_in

    def body(*refs):
        kern(*refs[:n_in], *refs[n_in + n_alias:])

    return pl.pallas_call(
        body,
        grid=(1,),
        in_specs=specs,
        out_specs=out_specs,
        out_shape=out_shapes,
        input_output_aliases=aliases,
        compiler_params=_params(("arbitrary",)),
        name=name,
    )(*args)


def _slab_shape(rows):
    return (rows, D_MODEL // LANES, LANES)


def _even_sample(x, prev, consts, pool, pool_out, cst, q_consts):
    ops = [x] + (list(prev) if prev else []) + list(consts) + [pool, cst] + list(q_consts)
    kern = functools.partial(_even_sample_kernel, has_prev=prev is not None)
    outs = [x.shape, _slab_shape(x.shape[0]), _StackedOut(pool[0].shape, pool[1], pool_out), cst[0].shape[1:]]
    return _sample_call(kern, ops, outs, "even_sample")


def _xattn_sample_stream_kernel(q_ref, k_ref, v_ref, o_ref, *, bt):
    slab_rows, lanes = q_ref.shape[1:]
    ones = jnp.ones((lanes, lanes), BF16)
    for b in range(bt):
        prod = (k_ref[b] * q_ref[b][None]).reshape(N_MEM * slab_rows, lanes)
        r = _dot(prod.astype(BF16), ones).reshape(N_MEM, slab_rows, lanes)
        s = r + pltpu.roll(r, XA_HEADS, 1)
        e = jnp.exp2(s - jnp.max(s, axis=0, keepdims=True))
        o_ref[b] = jnp.sum(e * v_ref[b], axis=0) / jnp.sum(e, axis=0)


def _xattn_sample_stream(q8, k8, v8, layer, bt):
    DB = q8.shape[0]
    slab = q8.shape[1:]
    kern = functools.partial(_xattn_sample_stream_kernel, bt=bt)
    return pl.pallas_call(
        kern,
        grid=(DB // bt,),
        in_specs=[pl.BlockSpec((bt,) + slab, lambda i: (i, 0, 0)),
                  pl.BlockSpec((None, bt, N_MEM) + slab, lambda i: (layer, i, 0, 0, 0)),
                  pl.BlockSpec((None, bt, N_MEM) + slab, lambda i: (layer, i, 0, 0, 0))],
        out_specs=pl.BlockSpec((bt,) + slab, lambda i: (i, 0, 0)),
        out_shape=jax.ShapeDtypeStruct(q8.shape, F32),
        compiler_params=_params(("arbitrary",)),
        name="xattn_sample_stream",
    )(q8, k8, v8)


def _xattn_sample_final_kernel(x_ref, o8_ref, wo_ref, gf_ref, xo_ref):
    xo_ref[...] = _rmsnorm(_finish_xattn(x_ref[...], o8_ref, wo_ref), gf_ref[...])


def _xattn_sample_final(x, o8, wo, gf):
    return _sample_call(_xattn_sample_final_kernel, (x, o8, wo, gf), [x.shape], "xattn_sample_final")[0]


def _odd_sample_in_kernel(x_ref, o8_ref, wo_ref, g_ref, wt_ref, cw_ref, cb_ref, dtb_ref, alog_ref,
                          sconv_ref, x1_ref, z_ref, xs_ref, bm_ref, cm_ref, xdt_ref, dec_ref, sconv_o_ref):
    x = _finish_xattn(x_ref[...], o8_ref, wo_ref)
    x1_ref[...] = x
    h = _rmsnorm(x, g_ref[...]).astype(BF16)
    xbc = _dot_nt(h, wt_ref[D_INNER:D_INNER + SSM_CONV_DIM, :])
    conv = cw_ref[3:4, :] * xbc
    for k in range(SSM_CONV_K - 1):
        conv = conv + cw_ref[k:k + 1, :] * sconv_ref[k]
    for k in range(SSM_CONV_K - 2):
        sconv_o_ref[k] = sconv_ref[k + 1]
    sconv_o_ref[SSM_CONV_K - 2] = xbc
    xc = _silu(conv + cb_ref[...])
    xs = xc[:, :D_INNER]
    dt = _softplus(_dt_raw(h, wt_ref) + dtb_ref[...])
    dta = dt * (-jnp.exp(alog_ref[...]))
    hrow = lax.broadcasted_iota(jnp.int32, (LANES, D_INNER), 0)
    ccol = lax.broadcasted_iota(jnp.int32, (LANES, D_INNER), 1)
    expand = jnp.where(ccol // SSM_HEAD_DIM == hrow, 1.0, 0.0).astype(BF16)
    z_ref[...] = _dot_nt(h, wt_ref[0:D_INNER, :])
    xs_ref[...] = xs
    bm_ref[...] = xc[:, D_INNER:D_INNER + SSM_GROUPS * SSM_STATE]
    cm_ref[...] = xc[:, D_INNER + SSM_GROUPS * SSM_STATE:]
    xdt_ref[...] = xs * _dot_exact_lhs(dt, expand)
    dec_ref[...] = jnp.exp(dta)


def _odd_sample_in(x, prev, consts, sconv, sconv_out):
    DB = x.shape[0]
    ops = [x] + list(prev) + list(consts) + [sconv]
    outs = [x.shape, (DB, D_INNER), (DB, D_INNER), (DB, SSM_GROUPS * SSM_STATE), (DB, SSM_GROUPS * SSM_STATE),
            (DB, D_INNER), (DB, LANES), _StackedOut(sconv[0].shape, sconv[1], sconv_out)]
    return _sample_call(_odd_sample_in_kernel, ops, outs, "odd_sample_in")


def _odd_sample_state_kernel(dec_ref, xdt_ref, bm_ref, cm_ref, s_ref, *rest, bt):
    so_ref, y_ref = rest[-2:]
    i = pl.program_id(0)
    pad = jnp.zeros((LANES - bt, D_INNER), F32)
    xdt_t = jnp.concatenate([xdt_ref[...], pad], axis=0).T.astype(BF16)
    padn = jnp.zeros((LANES - bt, SSM_STATE), F32)
    row = lax.broadcasted_iota(jnp.int32, (LANES, SSM_STATE), 0)
    y_acc = [jnp.zeros((SSM_GROUP_W, LANES), F32) for _ in range(SSM_GROUPS)]
    heads_per_group = SSM_GROUP_W // SSM_HEAD_DIM
    for g in range(SSM_GROUPS):
        nsl = slice(g * SSM_STATE, (g + 1) * SSM_STATE)
        gsl = slice(g * SSM_GROUP_W, (g + 1) * SSM_GROUP_W)
        b_all = jnp.concatenate([bm_ref[:, nsl], padn], axis=0)
        c_all = jnp.concatenate([cm_ref[:, nsl], padn], axis=0)
        for b in range(bt):
            b_sel = jnp.where(row == b, b_all, 0.0).astype(BF16)
            c_sel = jnp.where(row == b, c_all, 0.0).astype(BF16)
            upd = _dot(xdt_t[gsl, :], b_sel)
            pieces = []
            for e in range(heads_per_group):
                hd = g * heads_per_group + e
                rsl = slice(hd * SSM_HEAD_DIM, (hd + 1) * SSM_HEAD_DIM)
                new = s_ref[b, rsl, :] * dec_ref[i * bt + b, hd] + upd[e * SSM_HEAD_DIM:(e + 1) * SSM_HEAD_DIM, :]
                so_ref[b, rsl, :] = new
                pieces.append(new)
            y_acc[g] = y_acc[g] + _dot_nt(jnp.concatenate(pieces, axis=0).astype(BF16), c_sel)
    y_ref[...] = jnp.concatenate(y_acc, axis=0).T[0:bt, :]


def _odd_sample_state(dec, xdt, bm, cm, state_all, layer, prev_out, bt):
    DB = xdt.shape[0]
    kern = functools.partial(_odd_sample_state_kernel, bt=bt)
    gn = SSM_GROUPS * SSM_STATE
    in_specs = [pl.BlockSpec(memory_space=pltpu.SMEM),
                pl.BlockSpec((bt, D_INNER), lambda i: (i, 0)),
                pl.BlockSpec((bt, gn), lambda i: (i, 0)),
                pl.BlockSpec((bt, gn), lambda i: (i, 0)),
                pl.BlockSpec((None, bt, D_INNER, SSM_STATE), lambda i: (layer, i, 0, 0))]
    args = [dec, xdt, bm, cm, state_all]
    aliases = {}
    if prev_out is not None:
        in_specs.append(pl.BlockSpec(memory_space=pl.ANY))
        args.append(prev_out)
        aliases = {len(args) - 1: 0}
    return pl.pallas_call(
        kern,
        grid=(DB // bt,),
        in_specs=in_specs,
        out_specs=[pl.BlockSpec((None, bt, D_INNER, SSM_STATE), lambda i: (layer, i, 0, 0)),
                   pl.BlockSpec((bt, D_INNER), lambda i: (i, 0))],
        out_shape=[jax.ShapeDtypeStruct(state_all.shape, F32), jax.ShapeDtypeStruct((DB, D_INNER), F32)],
        input_output_aliases=aliases,
        compiler_params=_params(("arbitrary",)),
        name="odd_sample_state",
    )(*args)


def _odd_sample_out_kernel(x_ref, y_ref, xs_ref, z_ref, dskip_ref, nw_ref, w_out_ref, gq_ref, wq_ref,
                           xo_ref, q8_ref):
    yg = (y_ref[...] + dskip_ref[...] * xs_ref[...]) * _silu(z_ref[...])
    out = jnp.zeros(x_ref.shape, F32)
    for g in range(SSM_GROUPS):
        gsl = slice(g * SSM_GROUP_W, (g + 1) * SSM_GROUP_W)
        ygg = yg[:, gsl]
        yn = ygg * lax.rsqrt(jnp.mean(ygg * ygg, axis=-1, keepdims=True) + EPS) * nw_ref[:, gsl]
        out = out + _dot(yn.astype(BF16), w_out_ref[gsl, :])
    x = x_ref[...] + out
    xo_ref[...] = x
    _emit_query(x, gq_ref, wq_ref, q8_ref)


def _odd_sample_out(x, y, xs, z, consts, q_consts):
    ops = [x, y, xs, z] + list(consts) + list(q_consts)
    return _sample_call(_odd_sample_out_kernel, ops, [x.shape, _slab_shape(x.shape[0])], "odd_sample_out")


def _pad_lanes(v):
    return jnp.pad(v, ((0, 0), (0, LANES - v.shape[-1])))


def kernel(x_prompt, x_sample, mem_prompt, cache_mem_k, cache_mem_v, state_pool, state_conv, state_ssm_conv,
           state_ssm, norm_mix, norm_xa, norm_final, w_in_even, pool_w, pool_scale, conv_w, w_out_even, w_in_odd,
           ssm_conv_w, ssm_conv_b, dt_bias, a_log, d_skip, ssm_norm, w_out_odd, w_xa_q, w_xa_k, w_xa_v, w_xa_o):
    B, L, D = x_prompt.shape
    DB = x_sample.shape[0]
    depth = norm_mix.shape[0]
    tile = min(L, 512)
    tile_odd = min(L, 512)
    tile_xa = min(L, 1024)
    bt_x = min(DB, 8)
    bt_s = min(DB, 8)

    mem_k, mem_v, mem_k8, mem_v8 = _kv_proj(mem_prompt.reshape(B * N_MEM, D), w_xa_k, w_xa_v)
    mem_k = mem_k.reshape(depth, B, N_MEM, D)
    mem_v = mem_v.reshape(depth, B, N_MEM, D)
    slab = (D // LANES, LANES)
    cache_k8 = _to_head_slabs(cache_mem_k)
    cache_v8 = _to_head_slabs(cache_mem_v)
    ssm_all = state_ssm.reshape(state_ssm.shape[0], DB, D_INNER, SSM_STATE)
    ssm_s = None

    def rows(a):
        return a.reshape(a.shape[0], 1, a.shape[-1])

    g_mix_all, g_xa_all = rows(norm_mix), rows(norm_xa)
    w_in_e, pw_e, ps_e = w_in_even.astype(BF16), pool_w.astype(BF16), rows(pool_scale)
    w_out_e = w_out_even.astype(BF16)
    wt_o = jnp.swapaxes(w_in_odd, 1, 2).astype(BF16)
    cb_o, dtb_o, alog_o = rows(ssm_conv_b), rows(_pad_lanes(dt_bias)), rows(_pad_lanes(a_log))
    dskip_o, nw_o = rows(jnp.repeat(d_skip, SSM_HEAD_DIM, axis=1)), rows(ssm_norm)
    w_out_o = w_out_odd.astype(BF16)
    wq_all, wo_all = w_xa_q.astype(BF16), w_xa_o.astype(BF16)

    xp = x_prompt
    xs_ = x_sample.reshape(DB, D)
    gf = norm_final.reshape(1, D)
    pool_p, conv_p, conv_s, sconv_p, ssm_p = [], [], [], [], []
    pool_t, sconv_t = state_pool.transpose(0, 2, 1, 3), state_ssm_conv.transpose(0, 2, 1, 3)
    pool_s = sconv_s = None
    prev = None
    for l in range(depth):
        i = l // 2
        g_mix = _layer(g_mix_all, l)
        g_xa, wq, wo = _layer(g_xa_all, l), _layer(wq_all, l), _layer(wo_all, l)
        if l % 2 == 0:
            consts = [g_mix, _layer(w_in_e, i), _layer(pw_e, i), _layer(ps_e, i), _layer(conv_w, i),
                      _layer(w_out_e, i)]
            xp, pb, cbuf = _even_prompt(xp, consts, tile)
            pool_p.append(pb[:, HALO - POOL_BUF:, :])
            conv_p.append(cbuf[:, HALO - (CONV_K - 1):, :])
            xs_, q8, pool_s, cst_n = _even_sample(xs_, prev, consts, _layer(pool_t, i), pool_s,
                                                  _layer(state_conv, i), (g_xa, wq))
            conv_s.append(cst_n)
        else:
            in_consts = [g_mix, _layer(wt_o, i), _layer(ssm_conv_w, i),
                         _layer(cb_o, i), _layer(dtb_o, i), _layer(alog_o, i)]
            out_consts = [_layer(dskip_o, i), _layer(nw_o, i), _layer(w_out_o, i)]
            xp, sb, st = _odd_prompt(xp, in_consts + out_consts, tile_odd)
            sconv_p.append(sb[:, HALO - (SSM_CONV_K - 1):, :])
            ssm_p.append(st.reshape(B, SSM_HEADS, SSM_HEAD_DIM, SSM_STATE))
            xs_, z, xs_in, bm, cm, xdt, dec, sconv_s = _odd_sample_in(xs_, prev, in_consts,
                                                                      _layer(sconv_t, i), sconv_s)
            ssm_s, y = _odd_sample_state(dec, xdt, bm, cm, ssm_all, i, ssm_s, bt_s)
            xs_, q8 = _odd_sample_out(xs_, y, xs_in, z, out_consts, (g_xa, wq))
        final = l == depth - 1
        xp = _xattn_prompt(xp, [g_xa, wq, wo], mem_k, mem_v, l, gf, tile_xa, final)
        prev = (_xattn_sample_stream(q8, cache_k8, cache_v8, l, bt_x), wo)
    xs_ = _xattn_sample_final(xs_, prev[0], prev[1], gf)

    mem_shape = (depth, B, N_MEM) + slab
    return (xp, xs_.reshape(DB, 1, D),
            _from_head_slabs(mem_k8.reshape(mem_shape)), _from_head_slabs(mem_v8.reshape(mem_shape)),
            jnp.stack(pool_p), pool_s.transpose(0, 2, 1, 3), jnp.stack(conv_p), jnp.stack(conv_s),
            jnp.stack(sconv_p), sconv_s.transpose(0, 2, 1, 3), jnp.stack(ssm_p),
            ssm_s.reshape(ssm_all.shape[0], DB, SSM_HEADS, SSM_HEAD_DIM, SSM_STATE))
```

```python
import functools

import jax
import jax.numpy as jnp
from jax import lax
from jax.experimental import pallas as pl
from jax.experimental.pallas import tpu as pltpu

F32 = jnp.float32
BF16 = jnp.bfloat16

D_MODEL = 1024
PAST_LEN = 16384
POOL_WINDOWS = (2, 4, 8, 16)
POOL_GROUP = D_MODEL // len(POOL_WINDOWS)
POOL_BUF = max(POOL_WINDOWS) - 1
CONV_K = 3
D_INNER = 2 * D_MODEL
SSM_HEAD_DIM = 64
SSM_HEADS = D_INNER // SSM_HEAD_DIM
SSM_GROUPS = 4
SSM_STATE = 128
SSM_GROUP_W = D_INNER // SSM_GROUPS
SSM_CONV_K = 4
SSM_CONV_DIM = D_INNER + 2 * SSM_GROUPS * SSM_STATE
SSM_CHUNK = 128
N_MEM = 256
XA_HEADS = 4
XA_HEAD_DIM = D_MODEL // XA_HEADS
EPS = 1e-6
LOG2E = 1.4426950408889634
LANES = 128
HALO = 16
VMEM_LIMIT = 56 * 1024 * 1024


def _rmsnorm(x, g):
    return x * lax.rsqrt(jnp.mean(x * x, axis=-1, keepdims=True) + EPS) * g


def _silu(x):
    return x * jax.nn.sigmoid(x)


def _softplus(x):
    return jnp.maximum(x, 0.0) + jnp.log1p(jnp.exp(-jnp.abs(x)))


def _dot(a, b):
    return jnp.dot(a, b, preferred_element_type=F32)


def _dot_nt(a, b):
    return lax.dot_general(a, b, (((1,), (1,)), ((), ())), preferred_element_type=F32)


def _dt_raw(h, wt_ref):
    raw = _dot_nt(h, wt_ref[D_INNER + SSM_CONV_DIM:, :])
    return jnp.concatenate([raw, jnp.zeros((raw.shape[0], LANES - SSM_HEADS), F32)], axis=1)


def _split3(x):
    hi = x.astype(BF16)
    r1 = x - hi.astype(F32)
    mid = r1.astype(BF16)
    lo = (r1 - mid.astype(F32)).astype(BF16)
    return hi, mid, lo


def _dot_exact_rhs(a_bf16, x):
    hi, mid, lo = _split3(x)
    return _dot(a_bf16, hi) + _dot(a_bf16, mid) + _dot(a_bf16, lo)


def _dot_exact_lhs(x, b_bf16):
    hi, mid, lo = _split3(x)
    return _dot(hi, b_bf16) + _dot(mid, b_bf16) + _dot(lo, b_bf16)


def _const_spec(shape):
    nd = len(shape)
    return pl.BlockSpec(shape, lambda *_: (0,) * nd, pipeline_mode=pl.Buffered(1))


def _layer(arr, l):
    return (arr, l)


def _operands(ops):
    arrays, specs = [], []
    for op in ops:
        if isinstance(op, tuple):
            arr, l = op
            nd = arr.ndim - 1
            specs.append(pl.BlockSpec((None,) + arr.shape[1:], lambda *_, l=l, nd=nd: (l,) + (0,) * nd,
                                      pipeline_mode=pl.Buffered(1)))
        else:
            arr = op
            specs.append(_const_spec(arr.shape))
        arrays.append(arr)
    return arrays, specs


def _params(sem):
    return pltpu.CompilerParams(dimension_semantics=sem, vmem_limit_bytes=VMEM_LIMIT)


def _head_rows_order():
    lane_tiles = XA_HEAD_DIM // LANES
    return [(r % XA_HEADS) * lane_tiles + r // XA_HEADS for r in range(XA_HEADS * lane_tiles)]


def _to_head_slabs(a):
    lead = a.shape[:-2]
    lane_tiles = XA_HEAD_DIM // LANES
    nd = len(lead)
    a = a.reshape(lead + (XA_HEADS, lane_tiles, LANES))
    a = a.transpose(tuple(range(nd)) + (nd + 1, nd, nd + 2))
    return a.reshape(lead + (XA_HEADS * lane_tiles, LANES))


def _from_head_slabs(a):
    lead = a.shape[:-2]
    lane_tiles = XA_HEAD_DIM // LANES
    nd = len(lead)
    a = a.reshape(lead + (lane_tiles, XA_HEADS, LANES))
    a = a.transpose(tuple(range(nd)) + (nd + 1, nd, nd + 2))
    return a.reshape(lead + (XA_HEADS, XA_HEAD_DIM))


def _kv_proj_kernel(mem_ref, wk_ref, wv_ref, k_ref, v_ref, k8_ref, v8_ref):
    m = mem_ref[...].astype(BF16)
    k = _dot(m, wk_ref[0].astype(BF16))
    v = _dot(m, wv_ref[0].astype(BF16))
    k_ref[0] = k
    v_ref[0] = v
    for r, j in enumerate(_head_rows_order()):
        k8_ref[0, :, r, :] = k[:, j * LANES:(j + 1) * LANES]
        v8_ref[0, :, r, :] = v[:, j * LANES:(j + 1) * LANES]


def _kv_proj(mem2d, w_k, w_v):
    rows = mem2d.shape[0]
    depth = w_k.shape[0]
    tr = min(rows, 512)
    out = jax.ShapeDtypeStruct((depth, rows, D_MODEL), F32)
    out8 = jax.ShapeDtypeStruct((depth, rows, D_MODEL // LANES, LANES), F32)
    return pl.pallas_call(
        _kv_proj_kernel,
        grid=(depth, rows // tr),
        in_specs=[pl.BlockSpec((tr, D_MODEL), lambda l, r: (r, 0)),
                  pl.BlockSpec((1, D_MODEL, D_MODEL), lambda l, r: (l, 0, 0)),
                  pl.BlockSpec((1, D_MODEL, D_MODEL), lambda l, r: (l, 0, 0))],
        out_specs=[pl.BlockSpec((1, tr, D_MODEL), lambda l, r: (l, r, 0)),
                   pl.BlockSpec((1, tr, D_MODEL), lambda l, r: (l, r, 0)),
                   pl.BlockSpec((1, tr, D_MODEL // LANES, LANES), lambda l, r: (l, r, 0, 0)),
                   pl.BlockSpec((1, tr, D_MODEL // LANES, LANES), lambda l, r: (l, r, 0, 0))],
        out_shape=[out, out, out8, out8],
        compiler_params=_params(("arbitrary", "arbitrary")),
        name="kv_proj",
    )(mem2d, w_k, w_v)


def _even_prompt_kernel(x_ref, g_ref, w_in_ref, pw_ref, ps_ref, cw_ref, w_out_ref,
                        xo_ref, pool_ref, conv_ref, ext_ref, cext_ref, *, tile):
    t = pl.program_id(1)
    D = D_MODEL

    @pl.when(t == 0)
    def _():
        ext_ref[0:HALO, :] = jnp.zeros((HALO, D), F32)
        cext_ref[0:HALO, :] = jnp.zeros((HALO, D), F32)

    @pl.when(t > 0)
    def _():
        ext_ref[0:HALO, :] = ext_ref[tile:tile + HALO, :]
        cext_ref[0:HALO, :] = cext_ref[tile:tile + HALO, :]

    x = x_ref[0]
    h = _rmsnorm(x, g_ref[...]).astype(BF16)

    def proj(j):
        return _dot(h, w_in_ref[:, j * D:(j + 1) * D])

    u = proj(0)
    ext_ref[HALO:HALO + tile, :] = u
    g_pool = proj(1)
    pos = t * tile + lax.broadcasted_iota(jnp.int32, (tile, POOL_GROUP), 0)
    mixed = []
    for g, w in enumerate(POOL_WINDOWS):
        sl = slice(g * POOL_GROUP, (g + 1) * POOL_GROUP)
        s = ext_ref[:, sl]
        k = 1
        while k < w:
            s = s + pltpu.roll(s, k, 0)
            k *= 2
        cnt = jnp.minimum(pos + 1, w).astype(F32)
        diff = s[HALO:, :] / cnt - u[:, sl]
        yp = _dot(diff.astype(BF16), pw_ref[g]) * ps_ref[:, sl] * _silu(g_pool[:, sl])
        mixed.append(yp.astype(BF16))

    b_gate = proj(2)
    cv = proj(3) * proj(4)
    cext_ref[HALO:HALO + tile, :] = cv
    ce = cext_ref[...]
    conv = cw_ref[2:3, :] * cv
    conv = conv + cw_ref[1:2, :] * pltpu.roll(ce, 1, 0)[HALO:, :]
    conv = conv + cw_ref[0:1, :] * pltpu.roll(ce, 2, 0)[HALO:, :]
    yc = b_gate * conv * _silu(proj(5))
    mixed.append(yc.astype(BF16))
    xo_ref[0] = x + _dot(jnp.concatenate(mixed, axis=1), w_out_ref[...])

    @pl.when(t == pl.num_programs(1) - 1)
    def _():
        pool_ref[0] = ext_ref[tile:tile + HALO, :]
        conv_ref[0] = cext_ref[tile:tile + HALO, :]


def _even_prompt(x, consts, tile):
    B, L, D = x.shape
    kern = functools.partial(_even_prompt_kernel, tile=tile)
    arrays, specs = _operands(consts)
    return pl.pallas_call(
        kern,
        grid=(B, L // tile),
        in_specs=[pl.BlockSpec((1, tile, D), lambda b, t: (b, t, 0))] + specs,
        out_specs=[pl.BlockSpec((1, tile, D), lambda b, t: (b, t, 0)),
                   pl.BlockSpec((1, HALO, D), lambda b, t: (b, 0, 0)),
                   pl.BlockSpec((1, HALO, D), lambda b, t: (b, 0, 0))],
        out_shape=[jax.ShapeDtypeStruct((B, L, D), F32),
                   jax.ShapeDtypeStruct((B, HALO, D), F32),
                   jax.ShapeDtypeStruct((B, HALO, D), F32)],
        scratch_shapes=[pltpu.VMEM((HALO + tile, D), F32), pltpu.VMEM((HALO + tile, D), F32)],
        compiler_params=_params(("arbitrary", "arbitrary")),
        name="even_prompt",
    )(x, *arrays)


def _xattn_prompt_kernel(x_ref, g_ref, wq_ref, wo_ref, k_ref, v_ref, gf_ref, xo_ref, kb_ref, vb_ref, *, final):
    @pl.when(pl.program_id(1) == 0)
    def _():
        kb_ref[...] = k_ref[0].astype(BF16)
        vb_ref[...] = v_ref[0].astype(BF16)

    x = x_ref[0]
    h = _rmsnorm(x, g_ref[...]).astype(BF16)
    q = (_dot(h, wq_ref[...]) * (XA_HEAD_DIM ** -0.5 * LOG2E)).astype(BF16)
    heads = []
    for hd in range(XA_HEADS):
        sl = slice(hd * XA_HEAD_DIM, (hd + 1) * XA_HEAD_DIM)
        kh = kb_ref[:, sl]
        vh = vb_ref[:, sl]
        s = _dot_nt(q[:, sl], kh)
        e = jnp.exp2(s - jnp.max(s, axis=-1, keepdims=True))
        o = _dot(e.astype(BF16), vh) / jnp.sum(e, axis=-1, keepdims=True)
        heads.append(o.astype(BF16))
    y = x + _dot(jnp.concatenate(heads, axis=1), wo_ref[...])
    if final:
        y = _rmsnorm(y, gf_ref[...])
    xo_ref[0] = y


def _xattn_prompt(x, consts, k, v, layer, gf, tile, final):
    B, L, D = x.shape
    kern = functools.partial(_xattn_prompt_kernel, final=final)
    arrays, specs = _operands(consts)
    return pl.pallas_call(
        kern,
        grid=(B, L // tile),
        in_specs=[pl.BlockSpec((1, tile, D), lambda b, t: (b, t, 0))] + specs + [
            pl.BlockSpec((None, 1, N_MEM, D), lambda b, t: (layer, b, 0, 0)),
            pl.BlockSpec((None, 1, N_MEM, D), lambda b, t: (layer, b, 0, 0)),
            _const_spec(gf.shape)],
        out_specs=pl.BlockSpec((1, tile, D), lambda b, t: (b, t, 0)),
        out_shape=jax.ShapeDtypeStruct((B, L, D), F32),
        scratch_shapes=[pltpu.VMEM((N_MEM, D), BF16), pltpu.VMEM((N_MEM, D), BF16)],
        compiler_params=_params(("arbitrary", "arbitrary")),
        name="xattn_prompt",
    )(x, *arrays, k, v, gf)


def _odd_prompt_kernel(x_ref, g_ref, wt_ref, cw_ref, cb_ref, dtb_ref, alog_ref,
                       dskip_ref, nw_ref, w_out_ref,
                       xo_ref, sconv_ref, state_ref, xext_ref, st_ref, y_ref, *, tile):
    t = pl.program_id(1)
    Q = SSM_CHUNK
    C = SSM_CONV_DIM

    @pl.when(t == 0)
    def _():
        xext_ref[0:HALO, :] = jnp.zeros((HALO, C), F32)
        st_ref[...] = jnp.zeros(st_ref.shape, F32)

    @pl.when(t > 0)
    def _():
        xext_ref[0:HALO, :] = xext_ref[tile:tile + HALO, :]

    x = x_ref[0]
    h = _rmsnorm(x, g_ref[...]).astype(BF16)
    xbc = _dot_nt(h, wt_ref[D_INNER:D_INNER + SSM_CONV_DIM, :])
    xext_ref[HALO:HALO + tile, :] = xbc
    xe = xext_ref[...]
    xe2 = pltpu.roll(xe, 2, 0)
    odd_taps = cw_ref[2:3, :] * xe + cw_ref[0:1, :] * xe2
    conv = cw_ref[3:4, :] * xbc + cw_ref[1:2, :] * xe2[HALO:, :] + pltpu.roll(odd_taps, 1, 0)[HALO:, :]
    xc = _silu(conv + cb_ref[...])
    dt = _softplus(_dt_raw(h, wt_ref) + dtb_ref[...])
    dta = dt * (-jnp.exp(alog_ref[...]) * LOG2E)

    row = lax.broadcasted_iota(jnp.int32, (Q, Q), 0)
    col = lax.broadcasted_iota(jnp.int32, (Q, Q), 1)
    causal = row >= col
    tri = jnp.where(causal, 1.0, 0.0).astype(BF16)
    lo_lane = lax.broadcasted_iota(jnp.int32, (1, LANES), 1) < SSM_HEAD_DIM
    lo_tile = col < SSM_HEAD_DIM
    neg_inf = jnp.float32(-jnp.inf)

    for c in range(tile // Q):
        rs = slice(c * Q, (c + 1) * Q)
        a_cs = _dot_exact_rhs(tri, dta[rs, :])
        a_cs_t = a_cs.T
        dt_t = dt[rs, :].T
        w_t = dt_t * jnp.exp2(a_cs_t[:, Q - 1:Q] - a_cs_t)
        dt_b, w_b = dt_t.astype(BF16), w_t.astype(BF16)
        ea = jnp.exp2(a_cs)
        dec = ea[Q - 1:Q, :]
        for g in range(SSM_GROUPS):
            bsl = slice(D_INNER + g * SSM_STATE, D_INNER + (g + 1) * SSM_STATE)
            csl = slice(D_INNER + SSM_GROUPS * SSM_STATE + g * SSM_STATE,
                        D_INNER + SSM_GROUPS * SSM_STATE + (g + 1) * SSM_STATE)
            b_t = xc[rs, bsl].T
            c_g = xc[rs, csl].astype(BF16)
            b_t_b = b_t.astype(BF16)
            cb = _dot(c_g, b_t_b).astype(BF16)
            gsl = slice(g * SSM_GROUP_W, (g + 1) * SSM_GROUP_W)
            y_off = _dot(c_g, st_ref[:, gsl].astype(BF16))
            for j in range(SSM_GROUP_W // LANES):
                h0 = (g * SSM_GROUP_W + j * LANES) // SSM_HEAD_DIM
                lsl = slice(g * SSM_GROUP_W + j * LANES, g * SSM_GROUP_W + (j + 1) * LANES)
                xs = xc[rs, lsl]
                xs_b = xs.astype(BF16)
                zero_b = jnp.zeros_like(xs_b)
                wmat = jnp.concatenate([jnp.where(lo_tile, xs_b, zero_b), jnp.where(lo_tile, zero_b, xs_b)],
                                       axis=0)
                tops, bots = [], []
                for hh in (h0, h0 + 1):
                    seg = a_cs[:, hh:hh + 1] - a_cs_t[hh:hh + 1, :]
                    decay = jnp.exp2(jnp.where(causal, seg, neg_inf))
                    tops.append(cb * decay.astype(BF16) * dt_b[hh:hh + 1, :])
                    bots.append(b_t_b * w_b[hh:hh + 1, :])
                lhs = jnp.concatenate([jnp.concatenate(tops, axis=1), jnp.concatenate(bots, axis=1)],
                                      axis=0)
                r = _dot(lhs, wmat)
                escale = jnp.where(lo_lane, ea[:, h0:h0 + 1], ea[:, h0 + 1:h0 + 2])
                y_ref[rs, lsl] = (r[0:Q, :] + y_off[:, j * LANES:(j + 1) * LANES] * escale
                                  + dskip_ref[:, lsl] * xs)
                dscale = jnp.where(lo_lane, dec[:, h0:h0 + 1], dec[:, h0 + 1:h0 + 2])
                st_ref[:, lsl] = st_ref[:, lsl] * dscale + r[Q:2 * Q, :]

    yg = y_ref[...] * _silu(_dot_nt(h, wt_ref[0:D_INNER, :]))
    normed = []
    for g in range(SSM_GROUPS):
        gsl = slice(g * SSM_GROUP_W, (g + 1) * SSM_GROUP_W)
        ygg = yg[:, gsl]
        yn = ygg * lax.rsqrt(jnp.mean(ygg * ygg, axis=-1, keepdims=True) + EPS) * nw_ref[:, gsl]
        normed.append(yn.astype(BF16))
    xo_ref[0] = x + _dot(jnp.concatenate(normed, axis=1), w_out_ref[...])

    @pl.when(t == pl.num_programs(1) - 1)
    def _():
        sconv_ref[0] = xext_ref[tile:tile + HALO, :]
        state_ref[0] = st_ref[...].T


def _odd_prompt(x, consts, tile):
    B, L, D = x.shape
    kern = functools.partial(_odd_prompt_kernel, tile=tile)
    arrays, specs = _operands(consts)
    return pl.pallas_call(
        kern,
        grid=(B, L // tile),
        in_specs=[pl.BlockSpec((1, tile, D), lambda b, t: (b, t, 0))] + specs,
        out_specs=[pl.BlockSpec((1, tile, D), lambda b, t: (b, t, 0)),
                   pl.BlockSpec((1, HALO, SSM_CONV_DIM), lambda b, t: (b, 0, 0)),
                   pl.BlockSpec((1, D_INNER, SSM_STATE), lambda b, t: (b, 0, 0))],
        out_shape=[jax.ShapeDtypeStruct((B, L, D), F32),
                   jax.ShapeDtypeStruct((B, HALO, SSM_CONV_DIM), F32),
                   jax.ShapeDtypeStruct((B, D_INNER, SSM_STATE), F32)],
        scratch_shapes=[pltpu.VMEM((HALO + tile, SSM_CONV_DIM), F32),
                        pltpu.VMEM((SSM_STATE, D_INNER), F32),
                        pltpu.VMEM((tile, D_INNER), F32)],
        compiler_params=_params(("arbitrary", "arbitrary")),
        name="odd_prompt",
    )(x, *arrays)


def _finish_xattn(x, o8_ref, wo_ref):
    for r, j in enumerate(_head_rows_order()):
        x = x + _dot(o8_ref[:, r, :].astype(BF16), wo_ref[j * LANES:(j + 1) * LANES, :])
    return x


def _emit_query(x, g_ref, wq_ref, q8_ref):
    q = _dot(_rmsnorm(x, g_ref[...]).astype(BF16), wq_ref[...]) * (XA_HEAD_DIM ** -0.5 * LOG2E)
    for r, j in enumerate(_head_rows_order()):
        q8_ref[:, r, :] = q[:, j * LANES:(j + 1) * LANES]


def _even_sample_kernel(*refs, has_prev):
    refs = list(refs)
    x_ref = refs.pop(0)
    x = x_ref[...]
    if has_prev:
        o8_ref, wo_ref = refs.pop(0), refs.pop(0)
        x = _finish_xattn(x, o8_ref, wo_ref)
    (g_ref, w_in_ref, pw_ref, ps_ref, cw_ref, w_out_ref, pool_ref, cst_ref, gq_ref, wq_ref,
     xo_ref, q8_ref, pool_o_ref, cst_o_ref) = refs
    D = D_MODEL
    h = _rmsnorm(x, g_ref[...]).astype(BF16)

    def proj(j):
        return _dot(h, w_in_ref[:, j * D:(j + 1) * D])

    u = proj(0)
    g_pool = proj(1)
    out = jnp.zeros_like(x)
    for g, w in enumerate(POOL_WINDOWS):
        sl = slice(g * POOL_GROUP, (g + 1) * POOL_GROUP)
        s = u[:, sl]
        for j in range(POOL_BUF - (w - 1), POOL_BUF):
            s = s + pool_ref[j, :, sl]
        cnt = float(min(PAST_LEN + 1, w))
        diff = s / cnt - u[:, sl]
        yp = _dot(diff.astype(BF16), pw_ref[g]) * ps_ref[:, sl] * _silu(g_pool[:, sl])
        out = out + _dot(yp.astype(BF16), w_out_ref[sl, :])
    for j in range(POOL_BUF - 1):
        pool_o_ref[j] = pool_ref[j + 1]
    pool_o_ref[POOL_BUF - 1] = u

    b_gate = proj(2)
    cv = proj(3) * proj(4)
    conv = cw_ref[0:1, :] * cst_ref[:, 0, :] + cw_ref[1:2, :] * cst_ref[:, 1, :] + cw_ref[2:3, :] * cv
    cst_o_ref[:, 0, :] = cst_ref[:, 1, :]
    cst_o_ref[:, 1, :] = cv
    yc = b_gate * conv * _silu(proj(5))
    out = out + _dot(yc.astype(BF16), w_out_ref[D:2 * D, :])
    x = x + out
    xo_ref[...] = x
    _emit_query(x, gq_ref, wq_ref, q8_ref)


class _StackedOut:
    def __init__(self, shape, layer, prev):
        self.shape, self.layer, self.prev = shape, layer, prev


def _sample_call(kern, ops, outs, name):
    args, specs = _operands(ops)
    n_in = len(args)
    out_specs, out_shapes, aliases = [], [], {}
    for k, o in enumerate(outs):
        if isinstance(o, _StackedOut):
            nd = len(o.shape) - 1
            out_specs.append(pl.BlockSpec((None,) + o.shape[1:], lambda *_, l=o.layer, nd=nd: (l,) + (0,) * nd))
            out_shapes.append(jax.ShapeDtypeStruct(o.shape, F32))
            if o.prev is not None:
                aliases[len(args)] = k
                args.append(o.prev)
                specs.append(pl.BlockSpec(memory_space=pl.ANY))
        else:
            out_specs.append(_const_spec(o))
            out_shapes.append(jax.ShapeDtypeStruct(o, F32))
    n_alias = len(args) - n_in

    def body(*refs):
        kern(*refs[:n_in], *refs[n_in + n_alias:])

    return pl.pallas_call(
        body,
        grid=(1,),
        in_specs=specs,
        out_specs=out_specs,
        out_shape=out_shapes,
        input_output_aliases=aliases,
        compiler_params=_params(("arbitrary",)),
        name=name,
    )(*args)


def _slab_shape(rows):
    return (rows, D_MODEL // LANES, LANES)


def _even_sample(x, prev, consts, pool, pool_out, cst, q_consts):
    ops = [x] + (list(prev) if prev else []) + list(consts) + [pool, cst] + list(q_consts)
    kern = functools.partial(_even_sample_kernel, has_prev=prev is not None)
    outs = [x.shape, _slab_shape(x.shape[0]), _StackedOut(pool[0].shape, pool[1], pool_out), cst[0].shape[1:]]
    return _sample_call(kern, ops, outs, "even_sample")


def _xattn_sample_stream_kernel(q_ref, k_ref, v_ref, o_ref, *, bt):
    slab_rows, lanes = q_ref.shape[1:]
    ones = jnp.ones((lanes, lanes), BF16)
    for b in range(bt):
        prod = (k_ref[b] * q_ref[b][None]).reshape(N_MEM * slab_rows, lanes)
        r = _dot(prod.astype(BF16), ones).reshape(N_MEM, slab_rows, lanes)
        s = r + pltpu.roll(r, XA_HEADS, 1)
        e = jnp.exp2(s - jnp.max(s, axis=0, keepdims=True))
        o_ref[b] = jnp.sum(e * v_ref[b], axis=0) / jnp.sum(e, axis=0)


def _xattn_sample_stream(q8, k8, v8, layer, bt):
    DB = q8.shape[0]
    slab = q8.shape[1:]
    kern = functools.partial(_xattn_sample_stream_kernel, bt=bt)
    return pl.pallas_call(
        kern,
        grid=(DB // bt,),
        in_specs=[pl.BlockSpec((bt,) + slab, lambda i: (i, 0, 0)),
                  pl.BlockSpec((None, bt, N_MEM) + slab, lambda i: (layer, i, 0, 0, 0)),
                  pl.BlockSpec((None, bt, N_MEM) + slab, lambda i: (layer, i, 0, 0, 0))],
        out_specs=pl.BlockSpec((bt,) + slab, lambda i: (i, 0, 0)),
        out_shape=jax.ShapeDtypeStruct(q8.shape, F32),
        compiler_params=_params(("arbitrary",)),
        name="xattn_sample_stream",
    )(q8, k8, v8)


def _xattn_sample_final_kernel(x_ref, o8_ref, wo_ref, gf_ref, xo_ref):
    xo_ref[...] = _rmsnorm(_finish_xattn(x_ref[...], o8_ref, wo_ref), gf_ref[...])


def _xattn_sample_final(x, o8, wo, gf):
    return _sample_call(_xattn_sample_final_kernel, (x, o8, wo, gf), [x.shape], "xattn_sample_final")[0]


def _odd_sample_in_kernel(x_ref, o8_ref, wo_ref, g_ref, wt_ref, cw_ref, cb_ref, dtb_ref, alog_ref,
                          sconv_ref, x1_ref, z_ref, xs_ref, bm_ref, cm_ref, xdt_ref, dec_ref, sconv_o_ref):
    x = _finish_xattn(x_ref[...], o8_ref, wo_ref)
    x1_ref[...] = x
    h = _rmsnorm(x, g_ref[...]).astype(BF16)
    xbc = _dot_nt(h, wt_ref[D_INNER:D_INNER + SSM_CONV_DIM, :])
    conv = cw_ref[3:4, :] * xbc
    for k in range(SSM_CONV_K - 1):
        conv = conv + cw_ref[k:k + 1, :] * sconv_ref[k]
    for k in range(SSM_CONV_K - 2):
        sconv_o_ref[k] = sconv_ref[k + 1]
    sconv_o_ref[SSM_CONV_K - 2] = xbc
    xc = _silu(conv + cb_ref[...])
    xs = xc[:, :D_INNER]
    dt = _softplus(_dt_raw(h, wt_ref) + dtb_ref[...])
    dta = dt * (-jnp.exp(alog_ref[...]))
    hrow = lax.broadcasted_iota(jnp.int32, (LANES, D_INNER), 0)
    ccol = lax.broadcasted_iota(jnp.int32, (LANES, D_INNER), 1)
    expand = jnp.where(ccol // SSM_HEAD_DIM == hrow, 1.0, 0.0).astype(BF16)
    z_ref[...] = _dot_nt(h, wt_ref[0:D_INNER, :])
    xs_ref[...] = xs
    bm_ref[...] = xc[:, D_INNER:D_INNER + SSM_GROUPS * SSM_STATE]
    cm_ref[...] = xc[:, D_INNER + SSM_GROUPS * SSM_STATE:]
    xdt_ref[...] = xs * _dot_exact_lhs(dt, expand)
    dec_ref[...] = jnp.exp(dta)


def _odd_sample_in(x, prev, consts, sconv, sconv_out):
    DB = x.shape[0]
    ops = [x] + list(prev) + list(consts) + [sconv]
    outs = [x.shape, (DB, D_INNER), (DB, D_INNER), (DB, SSM_GROUPS * SSM_STATE), (DB, SSM_GROUPS * SSM_STATE),
            (DB, D_INNER), (DB, LANES), _StackedOut(sconv[0].shape, sconv[1], sconv_out)]
    return _sample_call(_odd_sample_in_kernel, ops, outs, "odd_sample_in")


def _odd_sample_state_kernel(dec_ref, xdt_ref, bm_ref, cm_ref, s_ref, *rest, bt):
    so_ref, y_ref = rest[-2:]
    i = pl.program_id(0)
    pad = jnp.zeros((LANES - bt, D_INNER), F32)
    xdt_t = jnp.concatenate([xdt_ref[...], pad], axis=0).T.astype(BF16)
    padn = jnp.zeros((LANES - bt, SSM_STATE), F32)
    row = lax.broadcasted_iota(jnp.int32, (LANES, SSM_STATE), 0)
    y_acc = [jnp.zeros((SSM_GROUP_W, LANES), F32) for _ in range(SSM_GROUPS)]
    heads_per_group = SSM_GROUP_W // SSM_HEAD_DIM
    for g in range(SSM_GROUPS):
        nsl = slice(g * SSM_STATE, (g + 1) * SSM_STATE)
        gsl = slice(g * SSM_GROUP_W, (g + 1) * SSM_GROUP_W)
        b_all = jnp.concatenate([bm_ref[:, nsl], padn], axis=0)
        c_all = jnp.concatenate([cm_ref[:, nsl], padn], axis=0)
        for b in range(bt):
            b_sel = jnp.where(row == b, b_all, 0.0).astype(BF16)
            c_sel = jnp.where(row == b, c_all, 0.0).astype(BF16)
            upd = _dot(xdt_t[gsl, :], b_sel)
            pieces = []
            for e in range(heads_per_group):
                hd = g * heads_per_group + e
                rsl = slice(hd * SSM_HEAD_DIM, (hd + 1) * SSM_HEAD_DIM)
                new = s_ref[b, rsl, :] * dec_ref[i * bt + b, hd] + upd[e * SSM_HEAD_DIM:(e + 1) * SSM_HEAD_DIM, :]
                so_ref[b, rsl, :] = new
                pieces.append(new)
            y_acc[g] = y_acc[g] + _dot_nt(jnp.concatenate(pieces, axis=0).astype(BF16), c_sel)
    y_ref[...] = jnp.concatenate(y_acc, axis=0).T[0:bt, :]


def _odd_sample_state(dec, xdt, bm, cm, state_all, layer, prev_out, bt):
    DB = xdt.shape[0]
    kern = functools.partial(_odd_sample_state_kernel, bt=bt)
    gn = SSM_GROUPS * SSM_STATE
    in_specs = [pl.BlockSpec(memory_space=pltpu.SMEM),
                pl.BlockSpec((bt, D_INNER), lambda i: (i, 0)),
                pl.BlockSpec((bt, gn), lambda i: (i, 0)),
                pl.BlockSpec((bt, gn), lambda i: (i, 0)),
                pl.BlockSpec((None, bt, D_INNER, SSM_STATE), lambda i: (layer, i, 0, 0))]
    args = [dec, xdt, bm, cm, state_all]
    aliases = {}
    if prev_out is not None:
        in_specs.append(pl.BlockSpec(memory_space=pl.ANY))
        args.append(prev_out)
        aliases = {len(args) - 1: 0}
    return pl.pallas_call(
        kern,
        grid=(DB // bt,),
        in_specs=in_specs,
        out_specs=[pl.BlockSpec((None, bt, D_INNER, SSM_STATE), lambda i: (layer, i, 0, 0)),
                   pl.BlockSpec((bt, D_INNER), lambda i: (i, 0))],
        out_shape=[jax.ShapeDtypeStruct(state_all.shape, F32), jax.ShapeDtypeStruct((DB, D_INNER), F32)],
        input_output_aliases=aliases,
        compiler_params=_params(("arbitrary",)),
        name="odd_sample_state",
    )(*args)


def _odd_sample_out_kernel(x_ref, y_ref, xs_ref, z_ref, dskip_ref, nw_ref, w_out_ref, gq_ref, wq_ref,
                           xo_ref, q8_ref):
    yg = (y_ref[...] + dskip_ref[...] * xs_ref[...]) * _silu(z_ref[...])
    out = jnp.zeros(x_ref.shape, F32)
    for g in range(SSM_GROUPS):
        gsl = slice(g * SSM_GROUP_W, (g + 1) * SSM_GROUP_W)
        ygg = yg[:, gsl]
        yn = ygg * lax.rsqrt(jnp.mean(ygg * ygg, axis=-1, keepdims=True) + EPS) * nw_ref[:, gsl]
        out = out + _dot(yn.astype(BF16), w_out_ref[gsl, :])
    x = x_ref[...] + out
    xo_ref[...] = x
    _emit_query(x, gq_ref, wq_ref, q8_ref)


def _odd_sample_out(x, y, xs, z, consts, q_consts):
    ops = [x, y, xs, z] + list(consts) + list(q_consts)
    return _sample_call(_odd_sample_out_kernel, ops, [x.shape, _slab_shape(x.shape[0])], "odd_sample_out")


def _pad_lanes(v):
    return jnp.pad(v, ((0, 0), (0, LANES - v.shape[-1])))


def kernel(x_prompt, x_sample, mem_prompt, cache_mem_k, cache_mem_v, state_pool, state_conv, state_ssm_conv,
           state_ssm, norm_mix, norm_xa, norm_final, w_in_even, pool_w, pool_scale, conv_w, w_out_even, w_in_odd,
           ssm_conv_w, ssm_conv_b, dt_bias, a_log, d_skip, ssm_norm, w_out_odd, w_xa_q, w_xa_k, w_xa_v, w_xa_o):
    B, L, D = x_prompt.shape
    DB = x_sample.shape[0]
    depth = norm_mix.shape[0]
    tile = min(L, 512)
    tile_odd = min(L, 512)
    tile_xa = min(L, 1024)
    bt_x = min(DB, 8)
    bt_s = min(DB, 8)

    mem_k, mem_v, mem_k8, mem_v8 = _kv_proj(mem_prompt.reshape(B * N_MEM, D), w_xa_k, w_xa_v)
    mem_k = mem_k.reshape(depth, B, N_MEM, D)
    mem_v = mem_v.reshape(depth, B, N_MEM, D)
    slab = (D // LANES, LANES)
    cache_k8 = _to_head_slabs(cache_mem_k)
    cache_v8 = _to_head_slabs(cache_mem_v)
    ssm_all = state_ssm.reshape(state_ssm.shape[0], DB, D_INNER, SSM_STATE)
    ssm_s = None

    def rows(a):
        return a.reshape(a.shape[0], 1, a.shape[-1])

    g_mix_all, g_xa_all = rows(norm_mix), rows(norm_xa)
    w_in_e, pw_e, ps_e = w_in_even.astype(BF16), pool_w.astype(BF16), rows(pool_scale)
    w_out_e = w_out_even.astype(BF16)
    wt_o = jnp.swapaxes(w_in_odd, 1, 2).astype(BF16)
    cb_o, dtb_o, alog_o = rows(ssm_conv_b), rows(_pad_lanes(dt_bias)), rows(_pad_lanes(a_log))
    dskip_o, nw_o = rows(jnp.repeat(d_skip, SSM_HEAD_DIM, axis=1)), rows(ssm_norm)
    w_out_o = w_out_odd.astype(BF16)
    wq_all, wo_all = w_xa_q.astype(BF16), w_xa_o.astype(BF16)

    xp = x_prompt
    xs_ = x_sample.reshape(DB, D)
    gf = norm_final.reshape(1, D)
    pool_p, conv_p, conv_s, sconv_p, ssm_p = [], [], [], [], []
    pool_t, sconv_t = state_pool.transpose(0, 2, 1, 3), state_ssm_conv.transpose(0, 2, 1, 3)
    pool_s = sconv_s = None
    prev = None
    for l in range(depth):
        i = l // 2
        g_mix = _layer(g_mix_all, l)
        g_xa, wq, wo = _layer(g_xa_all, l), _layer(wq_all, l), _layer(wo_all, l)
        if l % 2 == 0:
            consts = [g_mix, _layer(w_in_e, i), _layer(pw_e, i), _layer(ps_e, i), _layer(conv_w, i),
                      _layer(w_out_e, i)]
            xp, pb, cbuf = _even_prompt(xp, consts, tile)
            pool_p.append(pb[:, HALO - POOL_BUF:, :])
            conv_p.append(cbuf[:, HALO - (CONV_K - 1):, :])
            xs_, q8, pool_s, cst_n = _even_sample(xs_, prev, consts, _layer(pool_t, i), pool_s,
                                                  _layer(state_conv, i), (g_xa, wq))
            conv_s.append(cst_n)
        else:
            in_consts = [g_mix, _layer(wt_o, i), _layer(ssm_conv_w, i),
                         _layer(cb_o, i), _layer(dtb_o, i), _layer(alog_o, i)]
            out_consts = [_layer(dskip_o, i), _layer(nw_o, i), _layer(w_out_o, i)]
            xp, sb, st = _odd_prompt(xp, in_consts + out_consts, tile_odd)
            sconv_p.append(sb[:, HALO - (SSM_CONV_K - 1):, :])
            ssm_p.append(st.reshape(B, SSM_HEADS, SSM_HEAD_DIM, SSM_STATE))
            xs_, z, xs_in, bm, cm, xdt, dec, sconv_s = _odd_sample_in(xs_, prev, in_consts,
                                                                      _layer(sconv_t, i), sconv_s)
            ssm_s, y = _odd_sample_state(dec, xdt, bm, cm, ssm_all, i, ssm_s, bt_s)
            xs_, q8 = _odd_sample_out(xs_, y, xs_in, z, out_consts, (g_xa, wq))
        final = l == depth - 1
        xp = _xattn_prompt(xp, [g_xa, wq, wo], mem_k, mem_v, l, gf, tile_xa, final)
        prev = (_xattn_sample_stream(q8, cache_k8, cache_v8, l, bt_x), wo)
    xs_ = _xattn_sample_final(xs_, prev[0], prev[1], gf)

    mem_shape = (depth, B, N_MEM) + slab
    return (xp, xs_.reshape(DB, 1, D),
            _from_head_slabs(mem_k8.reshape(mem_shape)), _from_head_slabs(mem_v8.reshape(mem_shape)),
            jnp.stack(pool_p), pool_s.transpose(0, 2, 1, 3), jnp.stack(conv_p), jnp.stack(conv_s),
            jnp.stack(sconv_p), sconv_s.transpose(0, 2, 1, 3), jnp.stack(ssm_p),
            ssm_s.reshape(ssm_all.shape[0], DB, SSM_HEADS, SSM_HEAD_DIM, SSM_STATE))
```

```python
import functools

import jax
import jax.numpy as jnp
from jax import lax
from jax.experimental import pallas as pl
from jax.experimental.pallas import tpu as pltpu

F32 = jnp.float32
BF16 = jnp.bfloat16

D_MODEL = 1024
PAST_LEN = 16384
POOL_WINDOWS = (2, 4, 8, 16)
POOL_GROUP = D_MODEL // len(POOL_WINDOWS)
POOL_BUF = max(POOL_WINDOWS) - 1
CONV_K = 3
D_INNER = 2 * D_MODEL
SSM_HEAD_DIM = 64
SSM_HEADS = D_INNER // SSM_HEAD_DIM
SSM_GROUPS = 4
SSM_STATE = 128
SSM_GROUP_W = D_INNER // SSM_GROUPS
SSM_CONV_K = 4
SSM_CONV_DIM = D_INNER + 2 * SSM_GROUPS * SSM_STATE
SSM_CHUNK = 128
N_MEM = 256
XA_HEADS = 4
XA_HEAD_DIM = D_MODEL // XA_HEADS
EPS = 1e-6
LOG2E = 1.4426950408889634
LANES = 128
HALO = 16
VMEM_LIMIT = 56 * 1024 * 1024


def _rmsnorm(x, g):
    return x * lax.rsqrt(jnp.mean(x * x, axis=-1, keepdims=True) + EPS) * g


def _silu(x):
    return x * jax.nn.sigmoid(x)


def _softplus(x):
    return jnp.maximum(x, 0.0) + jnp.log1p(jnp.exp(-jnp.abs(x)))


def _dot(a, b):
    return jnp.dot(a, b, preferred_element_type=F32)


def _dot_nt(a, b):
    return lax.dot_general(a, b, (((1,), (1,)), ((), ())), preferred_element_type=F32)


def _dt_raw(h, wt_ref):
    raw = _dot_nt(h, wt_ref[D_INNER + SSM_CONV_DIM:, :])
    return jnp.concatenate([raw, jnp.zeros((raw.shape[0], LANES - SSM_HEADS), F32)], axis=1)


def _split3(x):
    hi = x.astype(BF16)
    r1 = x - hi.astype(F32)
    mid = r1.astype(BF16)
    lo = (r1 - mid.astype(F32)).astype(BF16)
    return hi, mid, lo


def _dot_exact_rhs(a_bf16, x):
    hi, mid, lo = _split3(x)
    return _dot(a_bf16, hi) + _dot(a_bf16, mid) + _dot(a_bf16, lo)


def _dot_exact_lhs(x, b_bf16):
    hi, mid, lo = _split3(x)
    return _dot(hi, b_bf16) + _dot(mid, b_bf16) + _dot(lo, b_bf16)


def _const_spec(shape):
    nd = len(shape)
    return pl.BlockSpec(shape, lambda *_: (0,) * nd, pipeline_mode=pl.Buffered(1))


def _layer(arr, l):
    return (arr, l)


def _operands(ops):
    arrays, specs = [], []
    for op in ops:
        if isinstance(op, tuple):
            arr, l = op
            nd = arr.ndim - 1
            specs.append(pl.BlockSpec((None,) + arr.shape[1:], lambda *_, l=l, nd=nd: (l,) + (0,) * nd,
                                      pipeline_mode=pl.Buffered(1)))
        else:
            arr = op
            specs.append(_const_spec(arr.shape))
        arrays.append(arr)
    return arrays, specs


def _params(sem):
    return pltpu.CompilerParams(dimension_semantics=sem, vmem_limit_bytes=VMEM_LIMIT)


def _head_rows_order():
    lane_tiles = XA_HEAD_DIM // LANES
    return [(r % XA_HEADS) * lane_tiles + r // XA_HEADS for r in range(XA_HEADS * lane_tiles)]


def _to_head_slabs(a):
    lead = a.shape[:-2]
    lane_tiles = XA_HEAD_DIM // LANES
    nd = len(lead)
    a = a.reshape(lead + (XA_HEADS, lane_tiles, LANES))
    a = a.transpose(tuple(range(nd)) + (nd + 1, nd, nd + 2))
    return a.reshape(lead + (XA_HEADS * lane_tiles, LANES))


def _from_head_slabs(a):
    lead = a.shape[:-2]
    lane_tiles = XA_HEAD_DIM // LANES
    nd = len(lead)
    a = a.reshape(lead + (lane_tiles, XA_HEADS, LANES))
    a = a.transpose(tuple(range(nd)) + (nd + 1, nd, nd + 2))
    return a.reshape(lead + (XA_HEADS, XA_HEAD_DIM))


def _kv_proj_kernel(mem_ref, wk_ref, wv_ref, k_ref, v_ref, k8_ref, v8_ref):
    m = mem_ref[...].astype(BF16)
    k = _dot(m, wk_ref[0].astype(BF16))
    v = _dot(m, wv_ref[0].astype(BF16))
    k_ref[0] = k.astype(BF16)
    v_ref[0] = v.astype(BF16)
    for r, j in enumerate(_head_rows_order()):
        k8_ref[0, :, r, :] = k[:, j * LANES:(j + 1) * LANES]
        v8_ref[0, :, r, :] = v[:, j * LANES:(j + 1) * LANES]


def _kv_proj(mem2d, w_k, w_v):
    rows = mem2d.shape[0]
    depth = w_k.shape[0]
    tr = min(rows, 512)
    out = jax.ShapeDtypeStruct((depth, rows, D_MODEL), BF16)
    out8 = jax.ShapeDtypeStruct((depth, rows, D_MODEL // LANES, LANES), F32)
    return pl.pallas_call(
        _kv_proj_kernel,
        grid=(depth, rows // tr),
        in_specs=[pl.BlockSpec((tr, D_MODEL), lambda l, r: (r, 0)),
                  pl.BlockSpec((1, D_MODEL, D_MODEL), lambda l, r: (l, 0, 0)),
                  pl.BlockSpec((1, D_MODEL, D_MODEL), lambda l, r: (l, 0, 0))],
        out_specs=[pl.BlockSpec((1, tr, D_MODEL), lambda l, r: (l, r, 0)),
                   pl.BlockSpec((1, tr, D_MODEL), lambda l, r: (l, r, 0)),
                   pl.BlockSpec((1, tr, D_MODEL // LANES, LANES), lambda l, r: (l, r, 0, 0)),
                   pl.BlockSpec((1, tr, D_MODEL // LANES, LANES), lambda l, r: (l, r, 0, 0))],
        out_shape=[out, out, out8, out8],
        compiler_params=_params(("arbitrary", "arbitrary")),
        name="kv_proj",
    )(mem2d, w_k, w_v)


def _even_prompt_kernel(x_ref, g_ref, w_in_ref, pw_ref, ps_ref, cw_ref, w_out_ref,
                        xo_ref, pool_ref, conv_ref, ext_ref, cext_ref, *, tile):
    t = pl.program_id(1)
    D = D_MODEL

    @pl.when(t == 0)
    def _():
        ext_ref[0:HALO, :] = jnp.zeros((HALO, D), F32)
        cext_ref[0:HALO, :] = jnp.zeros((HALO, D), F32)

    @pl.when(t > 0)
    def _():
        ext_ref[0:HALO, :] = ext_ref[tile:tile + HALO, :]
        cext_ref[0:HALO, :] = cext_ref[tile:tile + HALO, :]

    x = x_ref[0]
    h = _rmsnorm(x, g_ref[...]).astype(BF16)

    def proj(j):
        return _dot(h, w_in_ref[:, j * D:(j + 1) * D])

    u = proj(0)
    ext_ref[HALO:HALO + tile, :] = u
    g_pool = proj(1)
    pos = t * tile + lax.broadcasted_iota(jnp.int32, (tile, POOL_GROUP), 0)
    mixed = []
    for g, w in enumerate(POOL_WINDOWS):
        sl = slice(g * POOL_GROUP, (g + 1) * POOL_GROUP)
        s = ext_ref[:, sl]
        k = 1
        while k < w:
            s = s + pltpu.roll(s, k, 0)
            k *= 2
        cnt = jnp.minimum(pos + 1, w).astype(F32)
        diff = s[HALO:, :] / cnt - u[:, sl]
        yp = _dot(diff.astype(BF16), pw_ref[g]) * ps_ref[:, sl] * _silu(g_pool[:, sl])
        mixed.append(yp.astype(BF16))

    b_gate = proj(2)
    cv = proj(3) * proj(4)
    cext_ref[HALO:HALO + tile, :] = cv
    ce = cext_ref[...]
    conv = cw_ref[2:3, :] * cv
    conv = conv + cw_ref[1:2, :] * pltpu.roll(ce, 1, 0)[HALO:, :]
    conv = conv + cw_ref[0:1, :] * pltpu.roll(ce, 2, 0)[HALO:, :]
    yc = b_gate * conv * _silu(proj(5))
    mixed.append(yc.astype(BF16))
    xo_ref[0] = x + _dot(jnp.concatenate(mixed, axis=1), w_out_ref[...])

    @pl.when(t == pl.num_programs(1) - 1)
    def _():
        pool_ref[0] = ext_ref[tile:tile + HALO, :]
        conv_ref[0] = cext_ref[tile:tile + HALO, :]


def _even_prompt(x, consts, tile):
    B, L, D = x.shape
    kern = functools.partial(_even_prompt_kernel, tile=tile)
    arrays, specs = _operands(consts)
    return pl.pallas_call(
        kern,
        grid=(B, L // tile),
        in_specs=[pl.BlockSpec((1, tile, D), lambda b, t: (b, t, 0))] + specs,
        out_specs=[pl.BlockSpec((1, tile, D), lambda b, t: (b, t, 0)),
                   pl.BlockSpec((1, HALO, D), lambda b, t: (b, 0, 0)),
                   pl.BlockSpec((1, HALO, D), lambda b, t: (b, 0, 0))],
        out_shape=[jax.ShapeDtypeStruct((B, L, D), F32),
                   jax.ShapeDtypeStruct((B, HALO, D), F32),
                   jax.ShapeDtypeStruct((B, HALO, D), F32)],
        scratch_shapes=[pltpu.VMEM((HALO + tile, D), F32), pltpu.VMEM((HALO + tile, D), F32)],
        compiler_params=_params(("arbitrary", "arbitrary")),
        name="even_prompt",
    )(x, *arrays)


def _xattn_prompt_kernel(x_ref, g_ref, wq_ref, wo_ref, k_ref, v_ref, gf_ref, xo_ref, *, final):
    x = x_ref[0]
    h = _rmsnorm(x, g_ref[...]).astype(BF16)
    q = (_dot(h, wq_ref[...]) * (XA_HEAD_DIM ** -0.5 * LOG2E)).astype(BF16)
    heads = []
    for hd in range(XA_HEADS):
        sl = slice(hd * XA_HEAD_DIM, (hd + 1) * XA_HEAD_DIM)
        kh = k_ref[0, :, sl]
        vh = v_ref[0, :, sl]
        s = _dot_nt(q[:, sl], kh)
        e = jnp.exp2(s - jnp.max(s, axis=-1, keepdims=True))
        o = _dot(e.astype(BF16), vh) / jnp.sum(e, axis=-1, keepdims=True)
        heads.append(o.astype(BF16))
    y = x + _dot(jnp.concatenate(heads, axis=1), wo_ref[...])
    if final:
        y = _rmsnorm(y, gf_ref[...])
    xo_ref[0] = y


def _xattn_prompt(x, consts, k, v, layer, gf, tile, final):
    B, L, D = x.shape
    kern = functools.partial(_xattn_prompt_kernel, final=final)
    arrays, specs = _operands(consts)
    return pl.pallas_call(
        kern,
        grid=(B, L // tile),
        in_specs=[pl.BlockSpec((1, tile, D), lambda b, t: (b, t, 0))] + specs + [
            pl.BlockSpec((None, 1, N_MEM, D), lambda b, t: (layer, b, 0, 0)),
            pl.BlockSpec((None, 1, N_MEM, D), lambda b, t: (layer, b, 0, 0)),
            _const_spec(gf.shape)],
        out_specs=pl.BlockSpec((1, tile, D), lambda b, t: (b, t, 0)),
        out_shape=jax.ShapeDtypeStruct((B, L, D), F32),
        compiler_params=_params(("arbitrary", "arbitrary")),
        name="xattn_prompt",
    )(x, *arrays, k, v, gf)


def _odd_prompt_kernel(x_ref, g_ref, wt_ref, cw_ref, cb_ref, dtb_ref, alog_ref,
                       dskip_ref, nw_ref, w_out_ref,
                       xo_ref, sconv_ref, state_ref, xext_ref, st_ref, y_ref, *, tile):
    t = pl.program_id(1)
    Q = SSM_CHUNK
    C = SSM_CONV_DIM

    @pl.when(t == 0)
    def _():
        xext_ref[0:HALO, :] = jnp.zeros((HALO, C), F32)
        st_ref[...] = jnp.zeros(st_ref.shape, F32)

    @pl.when(t > 0)
    def _():
        xext_ref[0:HALO, :] = xext_ref[tile:tile + HALO, :]

    x = x_ref[0]
    h = _rmsnorm(x, g_ref[...]).astype(BF16)
    xbc = _dot_nt(h, wt_ref[D_INNER:D_INNER + SSM_CONV_DIM, :])
    xext_ref[HALO:HALO + tile, :] = xbc
    xe = xext_ref[...]
    xe2 = pltpu.roll(xe, 2, 0)
    odd_taps = cw_ref[2:3, :] * xe + cw_ref[0:1, :] * xe2
    conv = cw_ref[3:4, :] * xbc + cw_ref[1:2, :] * xe2[HALO:, :] + pltpu.roll(odd_taps, 1, 0)[HALO:, :]
    xc = _silu(conv + cb_ref[...])
    dt = _softplus(_dt_raw(h, wt_ref) + dtb_ref[...])
    dta = dt * (-jnp.exp(alog_ref[...]) * LOG2E)

    row = lax.broadcasted_iota(jnp.int32, (Q, Q), 0)
    col = lax.broadcasted_iota(jnp.int32, (Q, Q), 1)
    causal = row >= col
    tri = jnp.where(causal, 1.0, 0.0).astype(BF16)
    lo_lane = lax.broadcasted_iota(jnp.int32, (1, LANES), 1) < SSM_HEAD_DIM
    lo_tile = col < SSM_HEAD_DIM
    neg_inf = jnp.float32(-jnp.inf)

    for c in range(tile // Q):
        rs = slice(c * Q, (c + 1) * Q)
        a_cs = _dot_exact_rhs(tri, dta[rs, :])
        a_cs_t = a_cs.T
        dt_t = dt[rs, :].T
        w_t = dt_t * jnp.exp2(a_cs_t[:, Q - 1:Q] - a_cs_t)
        ea = jnp.exp2(a_cs)
        dec = ea[Q - 1:Q, :]
        for g in range(SSM_GROUPS):
            bsl = slice(D_INNER + g * SSM_STATE, D_INNER + (g + 1) * SSM_STATE)
            csl = slice(D_INNER + SSM_GROUPS * SSM_STATE + g * SSM_STATE,
                        D_INNER + SSM_GROUPS * SSM_STATE + (g + 1) * SSM_STATE)
            b_t = xc[rs, bsl].T
            c_g = xc[rs, csl].astype(BF16)
            cb = _dot(c_g, b_t.astype(BF16))
            gsl = slice(g * SSM_GROUP_W, (g + 1) * SSM_GROUP_W)
            y_off = _dot(c_g, st_ref[:, gsl].astype(BF16))
            for j in range(SSM_GROUP_W // LANES):
                h0 = (g * SSM_GROUP_W + j * LANES) // SSM_HEAD_DIM
                lsl = slice(g * SSM_GROUP_W + j * LANES, g * SSM_GROUP_W + (j + 1) * LANES)
                xs = xc[rs, lsl]
                xs_b = xs.astype(BF16)
                zero_b = jnp.zeros_like(xs_b)
                wmat = jnp.concatenate([jnp.where(lo_tile, xs_b, zero_b), jnp.where(lo_tile, zero_b, xs_b)],
                                       axis=0)
                tops, bots = [], []
                for hh in (h0, h0 + 1):
                    seg = a_cs[:, hh:hh + 1] - a_cs_t[hh:hh + 1, :]
                    decay = jnp.exp2(jnp.where(causal, seg, neg_inf))
                    tops.append(cb * decay * dt_t[hh:hh + 1, :])
                    bots.append(b_t * w_t[hh:hh + 1, :])
                lhs = jnp.concatenate([jnp.concatenate(tops, axis=1), jnp.concatenate(bots, axis=1)],
                                      axis=0).astype(BF16)
                r = _dot(lhs, wmat)
                escale = jnp.where(lo_lane, ea[:, h0:h0 + 1], ea[:, h0 + 1:h0 + 2])
                y_ref[rs, lsl] = (r[0:Q, :] + y_off[:, j * LANES:(j + 1) * LANES] * escale
                                  + dskip_ref[:, lsl] * xs)
                dscale = jnp.where(lo_lane, dec[:, h0:h0 + 1], dec[:, h0 + 1:h0 + 2])
                st_ref[:, lsl] = st_ref[:, lsl] * dscale + r[Q:2 * Q, :]

    yg = y_ref[...] * _silu(_dot_nt(h, wt_ref[0:D_INNER, :]))
    normed = []
    for g in range(SSM_GROUPS):
        gsl = slice(g * SSM_GROUP_W, (g + 1) * SSM_GROUP_W)
        ygg = yg[:, gsl]
        yn = ygg * lax.rsqrt(jnp.mean(ygg * ygg, axis=-1, keepdims=True) + EPS) * nw_ref[:, gsl]
        normed.append(yn.astype(BF16))
    xo_ref[0] = x + _dot(jnp.concatenate(normed, axis=1), w_out_ref[...])

    @pl.when(t == pl.num_programs(1) - 1)
    def _():
        sconv_ref[0] = xext_ref[tile:tile + HALO, :]
        state_ref[0] = st_ref[...].T


def _odd_prompt(x, consts, tile, layer, n_layers, states):
    B, L, D = x.shape
    kern = functools.partial(_odd_prompt_kernel, tile=tile)
    arrays, specs = _operands(consts)
    n_in = 1 + len(arrays)
    aliases = {}
    if states is not None:
        arrays.append(states)
        specs.append(pl.BlockSpec(memory_space=pl.ANY))
        aliases = {n_in: 2}

    def body(*refs):
        kern(*refs[:n_in], *refs[len(arrays) + 1:])

    return pl.pallas_call(
        body,
        grid=(B, L // tile),
        in_specs=[pl.BlockSpec((1, tile, D), lambda b, t: (b, t, 0))] + specs,
        out_specs=[pl.BlockSpec((1, tile, D), lambda b, t: (b, t, 0)),
                   pl.BlockSpec((1, HALO, SSM_CONV_DIM), lambda b, t: (b, 0, 0)),
                   pl.BlockSpec((None, 1, D_INNER, SSM_STATE), lambda b, t: (layer, b, 0, 0))],
        out_shape=[jax.ShapeDtypeStruct((B, L, D), F32),
                   jax.ShapeDtypeStruct((B, HALO, SSM_CONV_DIM), F32),
                   jax.ShapeDtypeStruct((n_layers, B, D_INNER, SSM_STATE), F32)],
        scratch_shapes=[pltpu.VMEM((HALO + tile, SSM_CONV_DIM), F32),
                        pltpu.VMEM((SSM_STATE, D_INNER), F32),
                        pltpu.VMEM((tile, D_INNER), F32)],
        input_output_aliases=aliases,
        compiler_params=_params(("arbitrary", "arbitrary")),
        name="odd_prompt",
    )(x, *arrays)


def _finish_xattn(x, o8_ref, wo_ref):
    for r, j in enumerate(_head_rows_order()):
        x = x + _dot(o8_ref[:, r, :].astype(BF16), wo_ref[j * LANES:(j + 1) * LANES, :])
    return x


def _emit_query(x, g_ref, wq_ref, q8_ref):
    q = _dot(_rmsnorm(x, g_ref[...]).astype(BF16), wq_ref[...]) * (XA_HEAD_DIM ** -0.5 * LOG2E)
    for r, j in enumerate(_head_rows_order()):
        q8_ref[:, r, :] = q[:, j * LANES:(j + 1) * LANES]


def _even_sample_kernel(*refs, has_prev):
    refs = list(refs)
    x_ref = refs.pop(0)
    x = x_ref[...]
    if has_prev:
        o8_ref, wo_ref = refs.pop(0), refs.pop(0)
        x = _finish_xattn(x, o8_ref, wo_ref)
    (g_ref, w_in_ref, pw_ref, ps_ref, cw_ref, w_out_ref, pool_ref, cst_ref, gq_ref, wq_ref,
     xo_ref, q8_ref, pool_o_ref, cst_o_ref) = refs
    D = D_MODEL
    h = _rmsnorm(x, g_ref[...]).astype(BF16)

    def proj(j):
        return _dot(h, w_in_ref[:, j * D:(j + 1) * D])

    u = proj(0)
    g_pool = proj(1)
    out = jnp.zeros_like(x)
    for g, w in enumerate(POOL_WINDOWS):
        sl = slice(g * POOL_GROUP, (g + 1) * POOL_GROUP)
        s = u[:, sl]
        for j in range(POOL_BUF - (w - 1), POOL_BUF):
            s = s + pool_ref[j, :, sl]
        cnt = float(min(PAST_LEN + 1, w))
        diff = s / cnt - u[:, sl]
        yp = _dot(diff.astype(BF16), pw_ref[g]) * ps_ref[:, sl] * _silu(g_pool[:, sl])
        out = out + _dot(yp.astype(BF16), w_out_ref[sl, :])
    for j in range(POOL_BUF - 1):
        pool_o_ref[j] = pool_ref[j + 1]
    pool_o_ref[POOL_BUF - 1] = u

    b_gate = proj(2)
    cv = proj(3) * proj(4)
    conv = cw_ref[0:1, :] * cst_ref[:, 0, :] + cw_ref[1:2, :] * cst_ref[:, 1, :] + cw_ref[2:3, :] * cv
    cst_o_ref[:, 0, :] = cst_ref[:, 1, :]
    cst_o_ref[:, 1, :] = cv
    yc = b_gate * conv * _silu(proj(5))
    out = out + _dot(yc.astype(BF16), w_out_ref[D:2 * D, :])
    x = x + out
    xo_ref[...] = x
    _emit_query(x, gq_ref, wq_ref, q8_ref)


class _StackedOut:
    def __init__(self, shape, layer, prev):
        self.shape, self.layer, self.prev = shape, layer, prev


def _sample_call(kern, ops, outs, name):
    args, specs = _operands(ops)
    n_in = len(args)
    out_specs, out_shapes, aliases = [], [], {}
    for k, o in enumerate(outs):
        if isinstance(o, _StackedOut):
            nd = len(o.shape) - 1
            out_specs.append(pl.BlockSpec((None,) + o.shape[1:], lambda *_, l=o.layer, nd=nd: (l,) + (0,) * nd))
            out_shapes.append(jax.ShapeDtypeStruct(o.shape, F32))
            if o.prev is not None:
                aliases[len(args)] = k
                args.append(o.prev)
                specs.append(pl.BlockSpec(memory_space=pl.ANY))
        else:
            out_specs.append(_const_spec(o))
            out_shapes.append(jax.ShapeDtypeStruct(o, F32))
    n_alias = len(args) - n_in

    def body(*refs):
        kern(*refs[:n_in], *refs[n_in + n_alias:])

    return pl.pallas_call(
        body,
        grid=(1,),
        in_specs=specs,
        out_specs=out_specs,
        out_shape=out_shapes,
        input_output_aliases=aliases,
        compiler_params=_params(("arbitrary",)),
        name=name,
    )(*args)


def _slab_shape(rows):
    return (rows, D_MODEL // LANES, LANES)


def _even_sample(x, prev, consts, pool, pool_out, cst, q_consts):
    ops = [x] + (list(prev) if prev else []) + list(consts) + [pool, cst] + list(q_consts)
    kern = functools.partial(_even_sample_kernel, has_prev=prev is not None)
    outs = [x.shape, _slab_shape(x.shape[0]), _StackedOut(pool[0].shape, pool[1], pool_out), cst[0].shape[1:]]
    return _sample_call(kern, ops, outs, "even_sample")


def _xattn_sample_stream_kernel(q_ref, k_ref, v_ref, o_ref, *, bt):
    slab_rows, lanes = q_ref.shape[1:]
    ones = jnp.ones((lanes, lanes), BF16)
    for b in range(bt):
        prod = (k_ref[b] * q_ref[b][None]).reshape(N_MEM * slab_rows, lanes)
        r = _dot(prod.astype(BF16), ones).reshape(N_MEM, slab_rows, lanes)
        s = r + pltpu.roll(r, XA_HEADS, 1)
        e = jnp.exp2(s - jnp.max(s, axis=0, keepdims=True))
        o_ref[b] = jnp.sum(e * v_ref[b], axis=0) / jnp.sum(e, axis=0)


def _xattn_sample_stream(q8, k8, v8, layer, bt):
    DB = q8.shape[0]
    slab = q8.shape[1:]
    kern = functools.partial(_xattn_sample_stream_kernel, bt=bt)
    return pl.pallas_call(
        kern,
        grid=(DB // bt,),
        in_specs=[pl.BlockSpec((bt,) + slab, lambda i: (i, 0, 0)),
                  pl.BlockSpec((None, bt, N_MEM) + slab, lambda i: (layer, i, 0, 0, 0)),
                  pl.BlockSpec((None, bt, N_MEM) + slab, lambda i: (layer, i, 0, 0, 0))],
        out_specs=pl.BlockSpec((bt,) + slab, lambda i: (i, 0, 0)),
        out_shape=jax.ShapeDtypeStruct(q8.shape, F32),
        compiler_params=_params(("arbitrary",)),
        name="xattn_sample_stream",
    )(q8, k8, v8)


def _xattn_sample_final_kernel(x_ref, o8_ref, wo_ref, gf_ref, xo_ref):
    xo_ref[...] = _rmsnorm(_finish_xattn(x_ref[...], o8_ref, wo_ref), gf_ref[...])


def _xattn_sample_final(x, o8, wo, gf):
    return _sample_call(_xattn_sample_final_kernel, (x, o8, wo, gf), [x.shape], "xattn_sample_final")[0]


def _odd_sample_in_kernel(x_ref, o8_ref, wo_ref, g_ref, wt_ref, cw_ref, cb_ref, dtb_ref, alog_ref,
                          sconv_ref, x1_ref, z_ref, xs_ref, bm_ref, cm_ref, xdt_ref, dec_ref, sconv_o_ref):
    x = _finish_xattn(x_ref[...], o8_ref, wo_ref)
    x1_ref[...] = x
    h = _rmsnorm(x, g_ref[...]).astype(BF16)
    xbc = _dot_nt(h, wt_ref[D_INNER:D_INNER + SSM_CONV_DIM, :])
    conv = cw_ref[3:4, :] * xbc
    for k in range(SSM_CONV_K - 1):
        conv = conv + cw_ref[k:k + 1, :] * sconv_ref[k]
    for k in range(SSM_CONV_K - 2):
        sconv_o_ref[k] = sconv_ref[k + 1]
    sconv_o_ref[SSM_CONV_K - 2] = xbc
    xc = _silu(conv + cb_ref[...])
    xs = xc[:, :D_INNER]
    dt = _softplus(_dt_raw(h, wt_ref) + dtb_ref[...])
    dta = dt * (-jnp.exp(alog_ref[...]))
    hrow = lax.broadcasted_iota(jnp.int32, (LANES, D_INNER), 0)
    ccol = lax.broadcasted_iota(jnp.int32, (LANES, D_INNER), 1)
    expand = jnp.where(ccol // SSM_HEAD_DIM == hrow, 1.0, 0.0).astype(BF16)
    z_ref[...] = _dot_nt(h, wt_ref[0:D_INNER, :])
    xs_ref[...] = xs
    bm_ref[...] = xc[:, D_INNER:D_INNER + SSM_GROUPS * SSM_STATE]
    cm_ref[...] = xc[:, D_INNER + SSM_GROUPS * SSM_STATE:]
    xdt_ref[...] = xs * _dot_exact_lhs(dt, expand)
    dec_ref[...] = jnp.exp(dta)


def _odd_sample_in(x, prev, consts, sconv, sconv_out):
    DB = x.shape[0]
    ops = [x] + list(prev) + list(consts) + [sconv]
    outs = [x.shape, (DB, D_INNER), (DB, D_INNER), (DB, SSM_GROUPS * SSM_STATE), (DB, SSM_GROUPS * SSM_STATE),
            (DB, D_INNER), (DB, LANES), _StackedOut(sconv[0].shape, sconv[1], sconv_out)]
    return _sample_call(_odd_sample_in_kernel, ops, outs, "odd_sample_in")


def _odd_sample_state_kernel(dec_ref, xdt_ref, bm_ref, cm_ref, s_ref, *rest, bt):
    so_ref, y_ref = rest[-2:]
    i = pl.program_id(0)
    pad = jnp.zeros((LANES - bt, D_INNER), F32)
    xdt_t = jnp.concatenate([xdt_ref[...], pad], axis=0).T.astype(BF16)
    padn = jnp.zeros((LANES - bt, SSM_STATE), F32)
    row = lax.broadcasted_iota(jnp.int32, (LANES, SSM_STATE), 0)
    y_acc = [jnp.zeros((SSM_GROUP_W, LANES), F32) for _ in range(SSM_GROUPS)]
    heads_per_group = SSM_GROUP_W // SSM_HEAD_DIM
    for g in range(SSM_GROUPS):
        nsl = slice(g * SSM_STATE, (g + 1) * SSM_STATE)
        gsl = slice(g * SSM_GROUP_W, (g + 1) * SSM_GROUP_W)
        b_all = jnp.concatenate([bm_ref[:, nsl], padn], axis=0)
        c_all = jnp.concatenate([cm_ref[:, nsl], padn], axis=0)
        for b in range(bt):
            b_sel = jnp.where(row == b, b_all, 0.0).astype(BF16)
            c_sel = jnp.where(row == b, c_all, 0.0).astype(BF16)
            upd = _dot(xdt_t[gsl, :], b_sel)
            pieces = []
            for e in range(heads_per_group):
                hd = g * heads_per_group + e
                rsl = slice(hd * SSM_HEAD_DIM, (hd + 1) * SSM_HEAD_DIM)
                new = s_ref[b, rsl, :] * dec_ref[i * bt + b, hd] + upd[e * SSM_HEAD_DIM:(e + 1) * SSM_HEAD_DIM, :]
                so_ref[b, rsl, :] = new
                pieces.append(new)
            y_acc[g] = y_acc[g] + _dot_nt(jnp.concatenate(pieces, axis=0).astype(BF16), c_sel)
    y_ref[...] = jnp.concatenate(y_acc, axis=0).T[0:bt, :]


def _odd_sample_state(dec, xdt, bm, cm, state_all, layer, prev_out, bt):
    DB = xdt.shape[0]
    kern = functools.partial(_odd_sample_state_kernel, bt=bt)
    gn = SSM_GROUPS * SSM_STATE
    in_specs = [pl.BlockSpec(memory_space=pltpu.SMEM),
                pl.BlockSpec((bt, D_INNER), lambda i: (i, 0)),
                pl.BlockSpec((bt, gn), lambda i: (i, 0)),
                pl.BlockSpec((bt, gn), lambda i: (i, 0)),
                pl.BlockSpec((None, bt, D_INNER, SSM_STATE), lambda i: (layer, i, 0, 0))]
    args = [dec, xdt, bm, cm, state_all]
    aliases = {}
    if prev_out is not None:
        in_specs.append(pl.BlockSpec(memory_space=pl.ANY))
        args.append(prev_out)
        aliases = {len(args) - 1: 0}
    return pl.pallas_call(
        kern,
        grid=(DB // bt,),
        in_specs=in_specs,
        out_specs=[pl.BlockSpec((None, bt, D_INNER, SSM_STATE), lambda i: (layer, i, 0, 0)),
                   pl.BlockSpec((bt, D_INNER), lambda i: (i, 0))],
        out_shape=[jax.ShapeDtypeStruct(state_all.shape, F32), jax.ShapeDtypeStruct((DB, D_INNER), F32)],
        input_output_aliases=aliases,
        compiler_params=_params(("arbitrary",)),
        name="odd_sample_state",
    )(*args)


def _odd_sample_out_kernel(x_ref, y_ref, xs_ref, z_ref, dskip_ref, nw_ref, w_out_ref, gq_ref, wq_ref,
                           xo_ref, q8_ref):
    yg = (y_ref[...] + dskip_ref[...] * xs_ref[...]) * _silu(z_ref[...])
    out = jnp.zeros(x_ref.shape, F32)
    for g in range(SSM_GROUPS):
        gsl = slice(g * SSM_GROUP_W, (g + 1) * SSM_GROUP_W)
        ygg = yg[:, gsl]
        yn = ygg * lax.rsqrt(jnp.mean(ygg * ygg, axis=-1, keepdims=True) + EPS) * nw_ref[:, gsl]
        out = out + _dot(yn.astype(BF16), w_out_ref[gsl, :])
    x = x_ref[...] + out
    xo_ref[...] = x
    _emit_query(x, gq_ref, wq_ref, q8_ref)


def _odd_sample_out(x, y, xs, z, consts, q_consts):
    ops = [x, y, xs, z] + list(consts) + list(q_consts)
    return _sample_call(_odd_sample_out_kernel, ops, [x.shape, _slab_shape(x.shape[0])], "odd_sample_out")


def _pad_lanes(v):
    return jnp.pad(v, ((0, 0), (0, LANES - v.shape[-1])))


def kernel(x_prompt, x_sample, mem_prompt, cache_mem_k, cache_mem_v, state_pool, state_conv, state_ssm_conv,
           state_ssm, norm_mix, norm_xa, norm_final, w_in_even, pool_w, pool_scale, conv_w, w_out_even, w_in_odd,
           ssm_conv_w, ssm_conv_b, dt_bias, a_log, d_skip, ssm_norm, w_out_odd, w_xa_q, w_xa_k, w_xa_v, w_xa_o):
    B, L, D = x_prompt.shape
    DB = x_sample.shape[0]
    depth = norm_mix.shape[0]
    tile = min(L, 512)
    tile_odd = min(L, 512)
    tile_xa = min(L, 1024)
    bt_x = min(DB, 8)
    bt_s = min(DB, 8)

    mem_k, mem_v, mem_k8, mem_v8 = _kv_proj(mem_prompt.reshape(B * N_MEM, D), w_xa_k, w_xa_v)
    mem_k = mem_k.reshape(depth, B, N_MEM, D)
    mem_v = mem_v.reshape(depth, B, N_MEM, D)
    slab = (D // LANES, LANES)
    cache_k8 = _to_head_slabs(cache_mem_k)
    cache_v8 = _to_head_slabs(cache_mem_v)
    ssm_all = state_ssm.reshape(state_ssm.shape[0], DB, D_INNER, SSM_STATE)
    ssm_s = None

    def rows(a):
        return a.reshape(a.shape[0], 1, a.shape[-1])

    g_mix_all, g_xa_all = rows(norm_mix), rows(norm_xa)
    w_in_e, pw_e, ps_e = w_in_even.astype(BF16), pool_w.astype(BF16), rows(pool_scale)
    w_out_e = w_out_even.astype(BF16)
    wt_o = jnp.swapaxes(w_in_odd, 1, 2).astype(BF16)
    cb_o, dtb_o, alog_o = rows(ssm_conv_b), rows(_pad_lanes(dt_bias)), rows(_pad_lanes(a_log))
    dskip_o, nw_o = rows(jnp.repeat(d_skip, SSM_HEAD_DIM, axis=1)), rows(ssm_norm)
    w_out_o = w_out_odd.astype(BF16)
    wq_all, wo_all = w_xa_q.astype(BF16), w_xa_o.astype(BF16)

    xp = x_prompt
    xs_ = x_sample.reshape(DB, D)
    gf = norm_final.reshape(1, D)
    pool_p, conv_p, conv_s, sconv_p = [], [], [], []
    ssm_p = None
    pool_t, sconv_t = state_pool.transpose(0, 2, 1, 3), state_ssm_conv.transpose(0, 2, 1, 3)
    pool_s = sconv_s = None
    prev = None
    for l in range(depth):
        i = l // 2
        g_mix = _layer(g_mix_all, l)
        g_xa, wq, wo = _layer(g_xa_all, l), _layer(wq_all, l), _layer(wo_all, l)
        if l % 2 == 0:
            consts = [g_mix, _layer(w_in_e, i), _layer(pw_e, i), _layer(ps_e, i), _layer(conv_w, i),
                      _layer(w_out_e, i)]
            xp, pb, cbuf = _even_prompt(xp, consts, tile)
            pool_p.append(pb[:, HALO - POOL_BUF:, :])
            conv_p.append(cbuf[:, HALO - (CONV_K - 1):, :])
            xs_, q8, pool_s, cst_n = _even_sample(xs_, prev, consts, _layer(pool_t, i), pool_s,
                                                  _layer(state_conv, i), (g_xa, wq))
            conv_s.append(cst_n)
        else:
            in_consts = [g_mix, _layer(wt_o, i), _layer(ssm_conv_w, i),
                         _layer(cb_o, i), _layer(dtb_o, i), _layer(alog_o, i)]
            out_consts = [_layer(dskip_o, i), _layer(nw_o, i), _layer(w_out_o, i)]
            xp, sb, ssm_p = _odd_prompt(xp, in_consts + out_consts, tile_odd, i, ssm_all.shape[0], ssm_p)
            sconv_p.append(sb[:, HALO - (SSM_CONV_K - 1):, :])
            xs_, z, xs_in, bm, cm, xdt, dec, sconv_s = _odd_sample_in(xs_, prev, in_consts,
                                                                      _layer(sconv_t, i), sconv_s)
            ssm_s, y = _odd_sample_state(dec, xdt, bm, cm, ssm_all, i, ssm_s, bt_s)
            xs_, q8 = _odd_sample_out(xs_, y, xs_in, z, out_consts, (g_xa, wq))
        final = l == depth - 1
        xp = _xattn_prompt(xp, [g_xa, wq, wo], mem_k, mem_v, l, gf, tile_xa, final)
        prev = (_xattn_sample_stream(q8, cache_k8, cache_v8, l, bt_x), wo)
    xs_ = _xattn_sample_final(xs_, prev[0], prev[1], gf)

    mem_shape = (depth, B, N_MEM) + slab
    return (xp, xs_.reshape(DB, 1, D),
            _from_head_slabs(mem_k8.reshape(mem_shape)), _from_head_slabs(mem_v8.reshape(mem_shape)),
            jnp.stack(pool_p), pool_s.transpose(0, 2, 1, 3), jnp.stack(conv_p), jnp.stack(conv_s),
            jnp.stack(sconv_p), sconv_s.transpose(0, 2, 1, 3),
            ssm_p.reshape(ssm_all.shape[0], B, SSM_HEADS, SSM_HEAD_DIM, SSM_STATE),
            ssm_s.reshape(ssm_all.shape[0], DB, SSM_HEADS, SSM_HEAD_DIM, SSM_STATE))
```
